```python
import jax, jax.numpy as jnp
from jax import lax
import numpy as np

D_MODEL = 1024
BATCH = 2
SEQ = 8192
DEPTH = 2

GRID_W = 64
CTX_LEN = 256
HEAD_DIM = 64
A_Q_HEADS = 6
A_KV_HEADS = 2
WINDOW = 128
BLOCK = 128
B_Q_HEADS = 6
B_KV_HEADS = 2
C_HEADS = 6
NA_KH_MAX = 8
NA_KW = 16
NA_QW = 16
NA_SLAB = NA_QW + NA_KW
N_BRANCH = 3
A_WIDTH = A_Q_HEADS * HEAD_DIM
B_WIDTH = B_Q_HEADS * HEAD_DIM
C_WIDTH = C_HEADS * HEAD_DIM
MLP_HIDDEN = 4 * D_MODEL
ROPE_THETA = 10000.0
NORM_EPS = 1e-6
NEG_INF = -1e30
SPLIT_SIZES = (A_Q_HEADS * HEAD_DIM, A_KV_HEADS * HEAD_DIM, A_KV_HEADS * HEAD_DIM,
               B_Q_HEADS * HEAD_DIM, B_KV_HEADS * HEAD_DIM, B_KV_HEADS * HEAD_DIM,
               C_HEADS * HEAD_DIM, C_HEADS * HEAD_DIM, C_HEADS * HEAD_DIM,
               N_BRANCH * D_MODEL)
IN_COLS = (A_Q_HEADS + 2 * A_KV_HEADS + B_Q_HEADS + 2 * B_KV_HEADS + 3 * C_HEADS) * HEAD_DIM + N_BRANCH * D_MODEL

kernel_name = "hybrid_parallel_gated_dit_block"


def rms_norm(x, gain):
    x32 = x.astype(jnp.float32)
    y = x32 * lax.rsqrt(jnp.mean(x32 * x32, axis=-1, keepdims=True) + NORM_EPS)
    return (y * gain.astype(jnp.float32)).astype(x.dtype)


def modulate(h, shift, scale):
    return h * (1 + scale) + shift


def axial_angles(n_tok):
    pos = jnp.arange(n_tok)
    row = (pos // GRID_W).astype(jnp.float32)
    col = (pos % GRID_W).astype(jnp.float32)
    n_freq = HEAD_DIM // 4
    freqs = ROPE_THETA ** (-jnp.arange(n_freq, dtype=jnp.float32) / n_freq)
    ang = jnp.concatenate([row[:, None] * freqs, col[:, None] * freqs], axis=-1)
    return jnp.cos(ang), jnp.sin(ang)


def rope_2d(x, cos, sin):
    xp = x.reshape(x.shape[:-1] + (HEAD_DIM // 2, 2))
    x0, x1 = xp[..., 0], xp[..., 1]
    c = cos[:, None, :].astype(x.dtype)
    s = sin[:, None, :].astype(x.dtype)
    return jnp.stack([x0 * c - x1 * s, x0 * s + x1 * c], axis=-1).reshape(x.shape)


def to_heads(t, n_heads):
    return t.reshape(t.shape[:2] + (n_heads, HEAD_DIM))


def split_cols(p):
    out = []
    start = 0
    for n in SPLIT_SIZES:
        out.append(p[..., start:start + n])
        start += n
    return out


def dense_context_attn(q, k, v, sink):
    bsz, lq, hkv, g, _ = q.shape
    s = jnp.einsum('bqkgd,bmkd->bkgqm', q, k).astype(jnp.float32) * (HEAD_DIM ** -0.5)
    if sink is not None:
        sink_col = jnp.broadcast_to(sink.astype(jnp.float32).reshape(hkv, g)[None, :, :, None, None], s.shape[:-1] + (1,))
        s = jnp.concatenate([s, sink_col], axis=-1)
    p = jax.nn.softmax(s, axis=-1)
    if sink is not None:
        p = p[..., :-1]
    o = jnp.einsum('bkgqm,bmkd->bqkgd', p.astype(v.dtype), v)
    return o.reshape(bsz, lq, hkv * g * HEAD_DIM)


def window_attention_latent(q, k, v, k_ctx, v_ctx, sink):
    bsz, seq = q.shape[:2]
    nb = seq // BLOCK
    g = A_Q_HEADS // A_KV_HEADS
    scale = HEAD_DIM ** -0.5
    qb = q.reshape(bsz, nb, BLOCK, A_KV_HEADS, g, HEAD_DIM)

    def band(t):
        tp = jnp.pad(t, ((0, 0), (BLOCK, BLOCK), (0, 0), (0, 0))).reshape(bsz, nb + 2, BLOCK, A_KV_HEADS, HEAD_DIM)
        return jnp.concatenate([tp[:, :-2], tp[:, 1:-1], tp[:, 2:]], axis=2)

    kb, vb = band(k), band(v)
    blk = jnp.arange(nb) * BLOCK
    qpos = blk[:, None, None] + jnp.arange(BLOCK)[None, :, None]
    kpos = (blk - BLOCK)[:, None, None] + jnp.arange(3 * BLOCK)[None, None, :]
    valid = (jnp.abs(qpos - kpos) <= WINDOW) & (kpos >= 0) & (kpos < seq)
    s_win = jnp.einsum('bnqkgd,bnmkd->bnkgqm', qb, kb).astype(jnp.float32) * scale
    s_win = jnp.where(valid[None, :, None, None], s_win, NEG_INF)
    s_ctx = jnp.einsum('bnqkgd,blkd->bnkgql', qb, k_ctx).astype(jnp.float32) * scale
    sink_col = jnp.broadcast_to(sink.astype(jnp.float32).reshape(A_KV_HEADS, g)[None, None, :, :, None, None], s_win.shape[:-1] + (1,))
    p = jax.nn.softmax(jnp.concatenate([s_win, s_ctx, sink_col], axis=-1), axis=-1)
    n_win = 3 * BLOCK
    n_ctx = k_ctx.shape[1]
    p_win = p[..., :n_win].astype(v.dtype)
    p_ctx = p[..., n_win:n_win + n_ctx].astype(v.dtype)
    o = jnp.einsum('bnkgqm,bnmkd->bnqkgd', p_win, vb) + jnp.einsum('bnkgql,blkd->bnqkgd', p_ctx, v_ctx)
    return o.reshape(bsz, seq, A_WIDTH)


def global_attention_latent(q, k, v, k_ctx, v_ctx):
    bsz, seq = q.shape[:2]
    nb = seq // BLOCK
    g = B_Q_HEADS // B_KV_HEADS
    scale = HEAD_DIM ** -0.5
    k_all = jnp.concatenate([k_ctx, k], axis=1)
    v_all = jnp.concatenate([v_ctx, v], axis=1)
    q_blocks = q.reshape(bsz, nb, BLOCK, B_KV_HEADS, g, HEAD_DIM).swapaxes(0, 1)

    def one_block(qb):
        s = jnp.einsum('bqkgd,bmkd->bkgqm', qb, k_all).astype(jnp.float32) * scale
        p = jax.nn.softmax(s, axis=-1).astype(v_all.dtype)
        return jnp.einsum('bkgqm,bmkd->bqkgd', p, v_all)

    o = lax.map(one_block, q_blocks)
    return o.swapaxes(0, 1).reshape(bsz, seq, B_WIDTH)


def neighbourhood_attention_latent(q, k, v, k_ctx, v_ctx, rpb):
    bsz, seq = q.shape[:2]
    rows = seq // GRID_W
    kh = min(NA_KH_MAX, rows)
    ncb = GRID_W // NA_QW
    scale = HEAD_DIM ** -0.5
    q_g = q.reshape(bsz, rows, ncb, NA_QW, C_HEADS, HEAD_DIM)
    k_g = k.reshape(bsz, rows, GRID_W, C_HEADS, HEAD_DIM)
    v_g = v.reshape(bsz, rows, GRID_W, C_HEADS, HEAD_DIM)
    r = jnp.arange(rows)
    row_start = jnp.clip(r - kh // 2, 0, rows - kh)
    row_idx = row_start[:, None] + jnp.arange(kh)[None, :]
    slab_start = jnp.clip(jnp.arange(ncb) * NA_QW - NA_KW // 2, 0, GRID_W - NA_SLAB)
    col_idx = slab_start[:, None] + jnp.arange(NA_SLAB)[None, :]

    def gather(t):
        return jnp.take(jnp.take(t, col_idx, axis=2), row_idx, axis=1)

    k_blk, v_blk = gather(k_g), gather(v_g)
    q_col = jnp.arange(ncb)[:, None] * NA_QW + jnp.arange(NA_QW)[None, :]
    win_start = jnp.clip(q_col - NA_KW // 2, 0, GRID_W - NA_KW)
    key_col = col_idx[:, None, :]
    col_valid = (key_col >= win_start[..., None]) & (key_col < win_start[..., None] + NA_KW)
    col_off = jnp.clip(key_col - q_col[..., None] + NA_KW - 1, 0, 2 * NA_KW - 2)
    row_off = row_idx - r[:, None] + NA_KH_MAX - 1
    bias = jnp.take(jnp.take(rpb, row_off, axis=1), col_off, axis=3)
    bias = bias.transpose(1, 3, 0, 4, 2, 5).astype(jnp.float32)
    s_nb = jnp.einsum('brcqhd,brkcshd->brchqks', q_g, k_blk).astype(jnp.float32) * scale + bias[None]
    s_nb = jnp.where(col_valid[None, None, :, None, :, None, :], s_nb, NEG_INF)
    s_ctx = jnp.einsum('brcqhd,blhd->brchql', q_g, k_ctx).astype(jnp.float32) * scale
    n_nb = kh * NA_SLAB
    p = jax.nn.softmax(jnp.concatenate([s_nb.reshape(s_nb.shape[:5] + (n_nb,)), s_ctx], axis=-1), axis=-1)
    p_nb = p[..., :n_nb].reshape(s_nb.shape).astype(v.dtype)
    p_ctx = p[..., n_nb:].astype(v.dtype)
    o = jnp.einsum('brchqks,brkcshd->brcqhd', p_nb, v_blk) + jnp.einsum('brchql,blhd->brcqhd', p_ctx, v_ctx)
    return o.reshape(bsz, seq, C_WIDTH)


def gated_merge(o_a, o_b, o_c, gates, w_br_a, w_br_b, w_br_c, w_out):
    g_a, g_b, g_c = jnp.split(gates, N_BRANCH, axis=-1)
    merged = (jax.nn.sigmoid(g_a) * (o_a @ w_br_a)
              + jax.nn.sigmoid(g_b) * (o_b @ w_br_b)
              + jax.nn.sigmoid(g_c) * (o_c @ w_br_c))
    return merged @ w_out


def token_mixer(h_lat, h_ctx, cos, sin, w_in, sink_a, qnorm_b, knorm_b, rpb_c,
                w_br_a, w_br_b, w_br_c, w_out, with_ctx_out):
    qa, ka, va, qb, kb, vb, qc, kc, vc, gates = split_cols(h_lat @ w_in)
    qa_c, ka_c, va_c, qb_c, kb_c, vb_c, qc_c, kc_c, vc_c, gates_c = split_cols(h_ctx @ w_in)
    bsz, n_ctx = h_ctx.shape[:2]
    qa = rope_2d(to_heads(qa, A_Q_HEADS), cos, sin)
    ka = rope_2d(to_heads(ka, A_KV_HEADS), cos, sin)
    va = to_heads(va, A_KV_HEADS)
    ka_c, va_c = to_heads(ka_c, A_KV_HEADS), to_heads(va_c, A_KV_HEADS)
    o_a = window_attention_latent(qa, ka, va, ka_c, va_c, sink_a)
    qb = rope_2d(rms_norm(to_heads(qb, B_Q_HEADS), qnorm_b), cos, sin)
    kb = rope_2d(rms_norm(to_heads(kb, B_KV_HEADS), knorm_b), cos, sin)
    vb = to_heads(vb, B_KV_HEADS)
    kb_c = rms_norm(to_heads(kb_c, B_KV_HEADS), knorm_b)
    vb_c = to_heads(vb_c, B_KV_HEADS)
    o_b = global_attention_latent(qb, kb, vb, kb_c, vb_c)
    qc, kc, vc = to_heads(qc, C_HEADS), to_heads(kc, C_HEADS), to_heads(vc, C_HEADS)
    kc_c, vc_c = to_heads(kc_c, C_HEADS), to_heads(vc_c, C_HEADS)
    o_c = neighbourhood_attention_latent(qc, kc, vc, kc_c, vc_c, rpb_c)
    y_lat = gated_merge(o_a, o_b, o_c, gates, w_br_a, w_br_b, w_br_c, w_out)
    if not with_ctx_out:
        return y_lat, None
    qa_c = to_heads(qa_c, A_Q_HEADS).reshape(bsz, n_ctx, A_KV_HEADS, A_Q_HEADS // A_KV_HEADS, HEAD_DIM)
    o_a_c = dense_context_attn(qa_c, ka_c, va_c, sink_a)
    qb_c = rms_norm(to_heads(qb_c, B_Q_HEADS), qnorm_b).reshape(bsz, n_ctx, B_KV_HEADS, B_Q_HEADS // B_KV_HEADS, HEAD_DIM)
    o_b_c = dense_context_attn(qb_c, kb_c, vb_c, None)
    qc_c = to_heads(qc_c, C_HEADS)[:, :, :, None, :]
    o_c_c = dense_context_attn(qc_c, kc_c, vc_c, None)
    y_ctx = gated_merge(o_a_c, o_b_c, o_c_c, gates_c, w_br_a, w_br_b, w_br_c, w_out)
    return y_lat, y_ctx


def sq_relu_mlp(h, w1, w2):
    a = jax.nn.relu(h @ w1)
    return (a * a) @ w2


def setup_inputs(seed: int = 0) -> dict:
    key = jax.random.key(seed)
    ks = jax.random.split(key, 24)

    def nrm(k, shape, scale):
        return jax.random.normal(k, shape, jnp.float32) * scale

    def gain(k, shape):
        return 1.0 + 0.05 * jax.random.normal(k, shape, jnp.float32)

    return {
        "x": nrm(ks[0], (BATCH, SEQ, D_MODEL), 1.0),
        "c": nrm(ks[1], (BATCH, D_MODEL), 1.0),
        "ctx": nrm(ks[2], (BATCH, CTX_LEN, D_MODEL), 1.0),
        "c_ctx": nrm(ks[3], (D_MODEL,), 1.0),
        "w_ada": nrm(ks[4], (DEPTH, D_MODEL, 6 * D_MODEL), 0.5 * D_MODEL ** -0.5),
        "b_ada": nrm(ks[5], (DEPTH, 6 * D_MODEL), 0.02),
        "norm_mix_pre": gain(ks[6], (DEPTH, D_MODEL)),
        "norm_mix_post": gain(ks[7], (DEPTH, D_MODEL)),
        "w_in": nrm(ks[8], (DEPTH, D_MODEL, IN_COLS), D_MODEL ** -0.5),
        "sink_a": nrm(ks[9], (DEPTH, A_Q_HEADS), 0.5),
        "qnorm_b": gain(ks[10], (DEPTH, HEAD_DIM)),
        "knorm_b": gain(ks[11], (DEPTH, HEAD_DIM)),
        "rpb_c": nrm(ks[12], (DEPTH, C_HEADS, 2 * NA_KH_MAX - 1, 2 * NA_KW - 1), 0.2),
        "w_br_a": nrm(ks[13], (DEPTH, A_WIDTH, D_MODEL), A_WIDTH ** -0.5),
        "w_br_b": nrm(ks[14], (DEPTH, B_WIDTH, D_MODEL), B_WIDTH ** -0.5),
        "w_br_c": nrm(ks[15], (DEPTH, C_WIDTH, D_MODEL), C_WIDTH ** -0.5),
        "w_out": nrm(ks[16], (DEPTH, D_MODEL, D_MODEL), D_MODEL ** -0.5),
        "norm_mlp_pre": gain(ks[17], (DEPTH, D_MODEL)),
        "norm_mlp_post": gain(ks[18], (DEPTH, D_MODEL)),
        "w_mlp_in": nrm(ks[19], (DEPTH, D_MODEL, MLP_HIDDEN), D_MODEL ** -0.5),
        "w_mlp_out": nrm(ks[20], (DEPTH, MLP_HIDDEN, D_MODEL), MLP_HIDDEN ** -0.5),
    }


def reference(x, c, ctx, c_ctx, w_ada, b_ada, norm_mix_pre, norm_mix_post, w_in, sink_a,
              qnorm_b, knorm_b, rpb_c, w_br_a, w_br_b, w_br_c, w_out, norm_mlp_pre,
              norm_mlp_post, w_mlp_in, w_mlp_out):
    seq = x.shape[1]
    cos, sin = axial_angles(seq)
    silu_c = jax.nn.silu(c)
    silu_cc = jax.nn.silu(c_ctx)
    x_lat, x_ctx = x, ctx
    for l in range(DEPTH):
        last = l == DEPTH - 1
        mod_lat = (silu_c @ w_ada[l] + b_ada[l])[:, None, :]
        mod_ctx = (silu_cc @ w_ada[l] + b_ada[l])[None, None, :]
        sh1, sc1, g1, sh2, sc2, g2 = jnp.split(mod_lat, 6, axis=-1)
        csh1, csc1, cg1, csh2, csc2, cg2 = jnp.split(mod_ctx, 6, axis=-1)
        h_lat = modulate(rms_norm(x_lat, norm_mix_pre[l]), sh1, sc1)
        h_ctx = modulate(rms_norm(x_ctx, norm_mix_pre[l]), csh1, csc1)
        y_lat, y_ctx = token_mixer(h_lat, h_ctx, cos, sin, w_in[l], sink_a[l], qnorm_b[l], knorm_b[l],
                                   rpb_c[l], w_br_a[l], w_br_b[l], w_br_c[l], w_out[l], not last)
        x_lat = x_lat + g1 * rms_norm(y_lat, norm_mix_post[l])
        h2 = modulate(rms_norm(x_lat, norm_mlp_pre[l]), sh2, sc2)
        x_lat = x_lat + g2 * rms_norm(sq_relu_mlp(h2, w_mlp_in[l], w_mlp_out[l]), norm_mlp_post[l])
        if not last:
            x_ctx = x_ctx + cg1 * rms_norm(y_ctx, norm_mix_post[l])
            h2c = modulate(rms_norm(x_ctx, norm_mlp_pre[l]), csh2, csc2)
            x_ctx = x_ctx + cg2 * rms_norm(sq_relu_mlp(h2c, w_mlp_in[l], w_mlp_out[l]), norm_mlp_post[l])
    return x_lat
```

```python
import jax
import jax.numpy as jnp
import numpy as np
from jax import lax
from jax.experimental import pallas as pl
from jax.experimental.pallas import tpu as pltpu

D_MODEL = 1024
SEQ = 8192
DEPTH = 2
GRID_W = 64
CTX_LEN = 256
HEAD_DIM = 64
A_Q_HEADS = 6
A_KV_HEADS = 2
WINDOW = 128
B_Q_HEADS = 6
B_KV_HEADS = 2
C_HEADS = 6
NA_KH_MAX = 8
NA_KW = 16
Q_WIDTH = 6 * HEAD_DIM
KV_WIDTH = 2 * HEAD_DIM
MLP_HIDDEN = 4 * D_MODEL
ROPE_THETA = 10000.0
NORM_EPS = 1e-6
NEG_INF = -1e30
SM_SCALE = HEAD_DIM ** -0.5

V7X_VMEM_LIMIT_BYTES = 56 * 1024 * 1024
LANES = 128
ROW_TILE = 256
KV_CHUNK = 512
ADA_COLS = 1536

BF16 = jnp.bfloat16
F32 = jnp.float32

A_OFF = 0
B_OFF = A_OFF + 1152
C_OFF = B_OFF + 1152
G_OFF = C_OFF + 1152
EXT_COLS = G_OFF + 3 * D_MODEL
QK_W = Q_WIDTH + KV_WIDTH


def _params(semantics):
    return pltpu.CompilerParams(dimension_semantics=semantics,
                                vmem_limit_bytes=V7X_VMEM_LIMIT_BYTES)


def _const_spec(shape):
    nd = len(shape)
    return pl.BlockSpec(shape, lambda *_: (0,) * nd, pipeline_mode=pl.Buffered(1))


def _dot(a, b):
    return jnp.dot(a, b, preferred_element_type=F32)


def _dot_nt(a, b):
    return lax.dot_general(a, b, (((1,), (1,)), ((), ())), preferred_element_type=F32)


def _rms(x):
    return x * lax.rsqrt(jnp.mean(x * x, axis=-1, keepdims=True) + NORM_EPS)


def _ada_kernel(c_ref, w_ref, b_ref, o_ref):
    c = c_ref[...]
    a = c / (1.0 + jnp.exp(-c))
    w = w_ref[0]
    a_hi = a.astype(BF16)
    a_lo = (a - a_hi.astype(F32)).astype(BF16)
    w_hi = w.astype(BF16)
    w_lo = (w - w_hi.astype(F32)).astype(BF16)
    acc = _dot(a_hi, w_hi) + _dot(a_hi, w_lo) + _dot(a_lo, w_hi)
    o_ref[0] = acc + b_ref[0]


def _ada_modulation(cvec, w_ada, b_ada):
    n_col = 6 * D_MODEL
    return pl.pallas_call(
        _ada_kernel,
        grid=(DEPTH, n_col // ADA_COLS),
        in_specs=[
            pl.BlockSpec((8, D_MODEL), lambda l, j: (0, 0)),
            pl.BlockSpec((1, D_MODEL, ADA_COLS), lambda l, j: (l, 0, j)),
            pl.BlockSpec((1, 1, ADA_COLS), lambda l, j: (l, 0, j)),
        ],
        out_specs=pl.BlockSpec((1, 8, ADA_COLS), lambda l, j: (l, 0, j)),
        out_shape=jax.ShapeDtypeStruct((DEPTH, 8, n_col), F32),
        compiler_params=_params(("arbitrary", "arbitrary")),
        name="ada_mod",
    )(cvec, w_ada, b_ada.reshape(DEPTH, 1, n_col))


def _proj_kernel(x_ref, mod_ref, gpre_ref, vecs_ref, cos_ref, sin_ref, gmat_ref, w_ref,
                 qa_ref, ka_ref, va_ref, qb_ref, kb_ref, vb_ref, qc_ref, kc_ref, vc_ref, sg_ref):
    x = x_ref[0]
    h = _rms(x) * gpre_ref[...]
    h = h * (1.0 + mod_ref[0, 1:2, :]) + mod_ref[0, 0:1, :]
    hb = h.astype(BF16)
    cos = cos_ref[...]
    sin = sin_ref[...]

    ra = _dot(hb, w_ref[:, A_OFF:A_OFF + 1152])
    for j in range(4):
        lo = j * LANES
        roped = ra[:, lo:lo + LANES] * cos + ra[:, QK_W + lo:QK_W + lo + LANES] * sin
        if j < 3:
            qa_ref[0, :, lo:lo + LANES] = roped.astype(BF16)
        else:
            ka_ref[0] = roped.astype(BF16)
    va_ref[0] = ra[:, 2 * QK_W:2 * QK_W + KV_WIDTH].astype(BF16)

    rb = _dot(hb, w_ref[:, B_OFF:B_OFF + 1152])
    xb = rb[:, 0:QK_W]
    sq = xb * xb
    sq_hi = sq.astype(BF16)
    sq_lo = (sq - sq_hi.astype(F32)).astype(BF16)
    head_ms = (_dot(sq_hi, gmat_ref[...]) + _dot(sq_lo, gmat_ref[...])) * (1.0 / HEAD_DIM)
    rinv = lax.rsqrt(head_ms + NORM_EPS)
    cq, sq_t = cos * vecs_ref[0:1, :], sin * vecs_ref[1:2, :]
    ck, sk_t = cos * vecs_ref[2:3, :], sin * vecs_ref[3:4, :]
    for j in range(4):
        lo = j * LANES
        c_t, s_t = (cq, sq_t) if j < 3 else (ck, sk_t)
        roped = rinv[:, lo:lo + LANES] * (rb[:, lo:lo + LANES] * c_t
                                          + rb[:, QK_W + lo:QK_W + lo + LANES] * s_t)
        if j < 3:
            qb_ref[0, :, lo:lo + LANES] = roped.astype(BF16)
        else:
            kb_ref[0, 0] = roped[:, 0:HEAD_DIM].astype(BF16)
            kb_ref[0, 1] = roped[:, HEAD_DIM:2 * HEAD_DIM].astype(BF16)
    vb = rb[:, 2 * QK_W:2 * QK_W + KV_WIDTH]
    vb_ref[0, 0] = vb[:, 0:HEAD_DIM].astype(BF16)
    vb_ref[0, 1] = vb[:, HEAD_DIM:2 * HEAD_DIM].astype(BF16)

    rc = _dot(hb, w_ref[:, C_OFF:C_OFF + 1152])
    qc_ref[0] = rc[:, 0:Q_WIDTH].astype(BF16)
    kc_ref[0] = rc[:, Q_WIDTH:2 * Q_WIDTH].astype(BF16)
    vc_ref[0] = rc[:, 2 * Q_WIDTH:3 * Q_WIDTH].astype(BF16)

    for j in range(3):
        lo = G_OFF + j * D_MODEL
        g = _dot(hb, w_ref[:, lo:lo + D_MODEL])
        sg_ref[0, :, j * D_MODEL:(j + 1) * D_MODEL] = (1.0 / (1.0 + jnp.exp(-g))).astype(BF16)


def _input_projection(x, mod, gpre, vecs, cos_t, sin_t, gmat, w_ext):
    bsz, n_tok, _ = x.shape
    tm = ROW_TILE
    row3 = lambda w: pl.BlockSpec((1, tm, w), lambda b, i: (b, i, 0))
    head4 = pl.BlockSpec((1, 2, tm, HEAD_DIM), lambda b, i: (b, 0, i, 0))
    bf = lambda *s: jax.ShapeDtypeStruct(s, BF16)
    return pl.pallas_call(
        _proj_kernel,
        grid=(bsz, n_tok // tm),
        in_specs=[
            row3(D_MODEL),
            pl.BlockSpec((1, 6, D_MODEL), lambda b, i: (b, 0, 0)),
            _const_spec((1, D_MODEL)),
            _const_spec((8, LANES)),
            pl.BlockSpec((tm, LANES), lambda b, i: (i, 0)),
            pl.BlockSpec((tm, LANES), lambda b, i: (i, 0)),
            _const_spec((QK_W, QK_W)),
            _const_spec((D_MODEL, EXT_COLS)),
        ],
        out_specs=[row3(Q_WIDTH), row3(KV_WIDTH), row3(KV_WIDTH),
                   row3(Q_WIDTH), head4, head4,
                   row3(Q_WIDTH), row3(Q_WIDTH), row3(Q_WIDTH),
                   row3(3 * D_MODEL)],
        out_shape=[bf(bsz, n_tok, Q_WIDTH), bf(bsz, n_tok, KV_WIDTH), bf(bsz, n_tok, KV_WIDTH),
                   bf(bsz, n_tok, Q_WIDTH), bf(bsz, 2, n_tok, HEAD_DIM), bf(bsz, 2, n_tok, HEAD_DIM),
                   bf(bsz, n_tok, Q_WIDTH), bf(bsz, n_tok, Q_WIDTH), bf(bsz, n_tok, Q_WIDTH),
                   bf(bsz, n_tok, 3 * D_MODEL)],
        compiler_params=_params(("arbitrary", "arbitrary")),
        name="in_proj",
    )(x, mod, gpre, vecs, cos_t, sin_t, gmat, w_ext)


def _joint_softmax_pv(scores, values, sink=None):
    m = None
    for s in scores:
        pm = jnp.max(s, axis=-1, keepdims=True)
        m = pm if m is None else jnp.maximum(m, pm)
    if sink is not None:
        m = jnp.maximum(m, sink)
    denom = None
    out = None
    for s, v in zip(scores, values):
        p = jnp.exp(s - m)
        ps = jnp.sum(p, axis=-1, keepdims=True)
        denom = ps if denom is None else denom + ps
        pv = _dot(p.astype(BF16), v)
        out = pv if out is None else out + pv
    if sink is not None:
        denom = denom + jnp.exp(sink - m)
    return out / denom


def _attn_a_kernel(sink_ref, q_ref, k0_ref, k1_ref, k2_ref, v0_ref, v1_ref, v2_ref,
                   kc_ref, vc_ref, o_ref):
    t = pl.program_id(1)
    nt = pl.num_programs(1)
    tq = ROW_TILE
    qi = lax.broadcasted_iota(jnp.int32, (tq, 3 * tq), 0)
    kj = lax.broadcasted_iota(jnp.int32, (tq, 3 * tq), 1)
    rel = kj - tq - qi
    valid = (rel <= WINDOW) & (rel >= -WINDOW)
    valid = valid & ((kj >= tq) | (t > 0)) & ((kj < 2 * tq) | (t < nt - 1))
    mask = jnp.where(valid, 0.0, NEG_INF).astype(F32)

    q = q_ref[0]
    kwin = jnp.concatenate([k0_ref[0], k1_ref[0], k2_ref[0]], axis=0)
    vwin = jnp.concatenate([v0_ref[0], v1_ref[0], v2_ref[0]], axis=0)
    kctx = kc_ref[0]
    vctx = vc_ref[0]
    group = A_Q_HEADS // A_KV_HEADS
    for h in range(A_Q_HEADS):
        g = h // group
        hs = slice(h * HEAD_DIM, (h + 1) * HEAD_DIM)
        gs = slice(g * HEAD_DIM, (g + 1) * HEAD_DIM)
        qh = q[:, hs]
        s_win = _dot_nt(qh, kwin[:, gs]) + mask
        s_ctx = _dot_nt(qh, kctx[:, gs])
        o = _joint_softmax_pv([s_win, s_ctx], [vwin[:, gs], vctx[:, gs]], sink=sink_ref[h])
        o_ref[0, :, hs] = o.astype(BF16)


def _attention_a(sink, q, k, v, k_ctx, v_ctx):
    bsz, n_tok, _ = q.shape
    tq = ROW_TILE
    nt = n_tok // tq
    prev = lambda b, i: (b, jnp.maximum(i - 1, 0), 0)
    cur = lambda b, i: (b, i, 0)
    nxt = lambda b, i: (b, jnp.minimum(i + 1, nt - 1), 0)
    kv = lambda im: pl.BlockSpec((1, tq, KV_WIDTH), im)
    ctx = pl.BlockSpec((1, CTX_LEN, KV_WIDTH), lambda b, i: (b, 0, 0))
    return pl.pallas_call(
        _attn_a_kernel,
        grid=(bsz, nt),
        in_specs=[pl.BlockSpec(memory_space=pltpu.SMEM),
                  pl.BlockSpec((1, tq, Q_WIDTH), cur),
                  kv(prev), kv(cur), kv(nxt), kv(prev), kv(cur), kv(nxt), ctx, ctx],
        out_specs=pl.BlockSpec((1, tq, Q_WIDTH), cur),
        out_shape=jax.ShapeDtypeStruct((bsz, n_tok, Q_WIDTH), BF16),
        compiler_params=_params(("arbitrary", "arbitrary")),
        name="attn_window",
    )(sink, q, k, k, k, v, v, v, k_ctx, v_ctx)


def _attn_b_kernel(q_ref, kc_ref, vc_ref, k_ref, v_ref, o_ref):
    tq = ROW_TILE
    n_chunks = k_ref.shape[2] // KV_CHUNK
    group = B_Q_HEADS // B_KV_HEADS
    q = q_ref[0]
    for g in range(B_KV_HEADS):
        q3 = jnp.concatenate(
            [q[:, (g * group + j) * HEAD_DIM:(g * group + j + 1) * HEAD_DIM] for j in range(group)],
            axis=0)
        s = _dot_nt(q3, kc_ref[0, g])
        m0 = jnp.max(s, axis=-1, keepdims=True)
        p = jnp.exp(s - m0)
        l0 = jnp.sum(p, axis=-1, keepdims=True)
        acc0 = _dot(p.astype(BF16), vc_ref[0, g])

        def body(c, carry, g=g, q3=q3):
            m, l, acc = carry
            start = pl.multiple_of(c * KV_CHUNK, KV_CHUNK)
            k = k_ref[0, g, pl.ds(start, KV_CHUNK), :]
            v = v_ref[0, g, pl.ds(start, KV_CHUNK), :]
            s = _dot_nt(q3, k)
            m_new = jnp.maximum(m, jnp.max(s, axis=-1, keepdims=True))
            alpha = jnp.exp(m - m_new)
            p = jnp.exp(s - m_new)
            l = alpha * l + jnp.sum(p, axis=-1, keepdims=True)
            acc = alpha * acc + _dot(p.astype(BF16), v)
            return m_new, l, acc

        _, l, acc = lax.fori_loop(0, n_chunks, body, (m0, l0, acc0))
        og = acc / l
        for j in range(group):
            h = g * group + j
            o_ref[0, :, h * HEAD_DIM:(h + 1) * HEAD_DIM] = og[j * tq:(j + 1) * tq].astype(BF16)


def _attention_b(q, k, v, k_ctx, v_ctx):
    bsz, n_tok, _ = q.shape
    tq = ROW_TILE
    full = lambda n: pl.BlockSpec((1, 2, n, HEAD_DIM), lambda b, i: (b, 0, 0, 0))
    return pl.pallas_call(
        _attn_b_kernel,
        grid=(bsz, n_tok // tq),
        in_specs=[pl.BlockSpec((1, tq, Q_WIDTH), lambda b, i: (b, i, 0)),
                  full(CTX_LEN), full(CTX_LEN), full(n_tok), full(n_tok)],
        out_specs=pl.BlockSpec((1, tq, Q_WIDTH), lambda b, i: (b, i, 0)),
        out_shape=jax.ShapeDtypeStruct((bsz, n_tok, Q_WIDTH), BF16),
        compiler_params=_params(("arbitrary", "arbitrary")),
        name="attn_global",
    )(q, k_ctx, v_ctx, k, v)


def _attn_c_kernel(tab_ref, q_ref, k0_ref, k1_ref, k2_ref, v0_ref, v1_ref, v2_ref,
                   kc_ref, vc_ref, o_ref):
    q = q_ref[0]
    kwin = jnp.concatenate([k0_ref[0], k1_ref[0], k2_ref[0]], axis=0)
    vwin = jnp.concatenate([v0_ref[0], v1_ref[0], v2_ref[0]], axis=0)
    kctx = kc_ref[0]
    vctx = vc_ref[0]
    for h in range(C_HEADS):
        hs = slice(h * HEAD_DIM, (h + 1) * HEAD_DIM)
        qh = q[:, hs]
        s_win = _dot_nt(qh, kwin[:, hs]) + tab_ref[0, h]
        s_ctx = _dot_nt(qh, kctx[:, hs])
        o = _joint_softmax_pv([s_win, s_ctx], [vwin[:, hs], vctx[:, hs]])
        o_ref[0, :, hs] = o.astype(BF16)


def _attention_c(tab, q, k, v, k_ctx, v_ctx):
    bsz, n_tok, _ = q.shape
    tq = ROW_TILE
    nt = n_tok // tq
    prev = lambda b, i: (b, jnp.maximum(i - 1, 0), 0)
    cur = lambda b, i: (b, i, 0)
    nxt = lambda b, i: (b, jnp.minimum(i + 1, nt - 1), 0)
    kv = lambda im: pl.BlockSpec((1, tq, Q_WIDTH), im)
    ctx = pl.BlockSpec((1, CTX_LEN, Q_WIDTH), lambda b, i: (b, 0, 0))
    variant = lambda b, i: (jnp.where(i == 0, 0, jnp.where(i == nt - 1, 2, 1)), 0, 0, 0)
    return pl.pallas_call(
        _attn_c_kernel,
        grid=(bsz, nt),
        in_specs=[pl.BlockSpec((1, C_HEADS, tq, 3 * tq), variant),
                  pl.BlockSpec((1, tq, Q_WIDTH), cur),
                  kv(prev), kv(cur), kv(nxt), kv(prev), kv(cur), kv(nxt), ctx, ctx],
        out_specs=pl.BlockSpec((1, tq, Q_WIDTH), cur),
        out_shape=jax.ShapeDtypeStruct((bsz, n_tok, Q_WIDTH), BF16),
        compiler_params=_params(("arbitrary", "arbitrary")),
        name="attn_nbr",
    )(tab, q, k, k, k, v, v, v, k_ctx, v_ctx)


def _neighbourhood_tables(rpb):
    rows = SEQ // GRID_W
    rpt = ROW_TILE // GRID_W
    nt = rows // rpt
    kh = min(NA_KH_MAX, rows)
    c = np.arange(GRID_W)
    ws = np.clip(c - NA_KW // 2, 0, GRID_W - NA_KW)
    valid_c = (c[None, :] >= ws[:, None]) & (c[None, :] < ws[:, None] + NA_KW)
    coff = np.clip(c[None, :] - c[:, None] + NA_KW - 1, 0, 2 * NA_KW - 2)
    tabs = []
    for t in (0, 1, nt - 1):
        r = rpt * t + np.arange(rpt)
        kr = (rpt * (t - 1 + np.arange(3)))[:, None] + np.arange(rpt)[None, :]
        rs = np.clip(r - kh // 2, 0, rows - kh)
        valid_r = (kr[None] >= rs[:, None, None]) & (kr[None] < rs[:, None, None] + kh)
        roff = np.clip(kr[None] - r[:, None, None] + NA_KH_MAX - 1, 0, 2 * NA_KH_MAX - 2)
        bias = rpb[:, roff][..., coff]
        bias = bias.transpose(0, 1, 4, 2, 3, 5)
        valid = valid_r[:, None, :, :, None] & valid_c[None, :, None, None, :]
        tabs.append(jnp.where(valid[None], bias, NEG_INF).reshape(C_HEADS, ROW_TILE, 3 * ROW_TILE))
    return jnp.stack(tabs).astype(F32)


def _attn_ctx_kernel(sink_ref, qa_ref, ka_ref, va_ref, qb_ref, kb_ref, vb_ref,
                     qc_ref, kc_ref, vc_ref, oa_ref, ob_ref, oc_ref):
    group = A_Q_HEADS // A_KV_HEADS
    qa, ka, va = qa_ref[0], ka_ref[0], va_ref[0]
    qb = qb_ref[0]
    qc, kc, vc = qc_ref[0], kc_ref[0], vc_ref[0]
    for h in range(A_Q_HEADS):
        g = h // group
        hs = slice(h * HEAD_DIM, (h + 1) * HEAD_DIM)
        gs = slice(g * HEAD_DIM, (g + 1) * HEAD_DIM)
        o = _joint_softmax_pv([_dot_nt(qa[:, hs], ka[:, gs])], [va[:, gs]], sink=sink_ref[h])
        oa_ref[0, :, hs] = o.astype(BF16)
        o = _joint_softmax_pv([_dot_nt(qb[:, hs], kb_ref[0, g])], [vb_ref[0, g]])
        ob_ref[0, :, hs] = o.astype(BF16)
        o = _joint_softmax_pv([_dot_nt(qc[:, hs], kc[:, hs])], [vc[:, hs]])
        oc_ref[0, :, hs] = o.astype(BF16)


def _attention_ctx(sink, qa, ka, va, qb, kb, vb, qc, kc, vc):
    bsz = qa.shape[0]
    row = lambda w: pl.BlockSpec((1, CTX_LEN, w), lambda b: (b, 0, 0))
    head = pl.BlockSpec((1, 2, CTX_LEN, HEAD_DIM), lambda b: (b, 0, 0, 0))
    out = jax.ShapeDtypeStruct((bsz, CTX_LEN, Q_WIDTH), BF16)
    return pl.pallas_call(
        _attn_ctx_kernel,
        grid=(bsz,),
        in_specs=[pl.BlockSpec(memory_space=pltpu.SMEM),
                  row(Q_WIDTH), row(KV_WIDTH), row(KV_WIDTH),
                  row(Q_WIDTH), head, head,
                  row(Q_WIDTH), row(Q_WIDTH), row(Q_WIDTH)],
        out_specs=[row(Q_WIDTH)] * 3,
        out_shape=[out] * 3,
        compiler_params=_params(("arbitrary",)),
        name="attn_ctx",
    )(sink, qa, ka, va, qb, kb, vb, qc, kc, vc)


def _post_kernel(x_ref, oa_ref, ob_ref, oc_ref, sg_ref, mod_ref, norms_ref,
                 wa_ref, wb_ref, wc_ref, wo_ref, w1_ref, w2_ref, o_ref):
    x = x_ref[0]
    merged = None
    for j, (o_r, w_r) in enumerate(((oa_ref, wa_ref), (ob_ref, wb_ref), (oc_ref, wc_ref))):
        gate = sg_ref[0, :, j * D_MODEL:(j + 1) * D_MODEL].astype(F32)
        term = gate * _dot(o_r[0], w_r[...])
        merged = term if merged is None else merged + term
    y = _dot(merged.astype(BF16), wo_ref[...])
    x1 = x + mod_ref[0, 2:3, :] * (_rms(y) * norms_ref[0:1, :])
    h2 = _rms(x1) * norms_ref[1:2, :]
    h2 = (h2 * (1.0 + mod_ref[0, 4:5, :]) + mod_ref[0, 3:4, :]).astype(BF16)
    z = None
    n_chunk = MLP_HIDDEN // D_MODEL
    for j in range(n_chunk):
        cs = slice(j * D_MODEL, (j + 1) * D_MODEL)
        a = jnp.maximum(_dot(h2, w1_ref[:, cs]), 0.0)
        zj = _dot((a * a).astype(BF16), w2_ref[cs, :])
        z = zj if z is None else z + zj
    o_ref[0] = x1 + mod_ref[0, 5:6, :] * (_rms(z) * norms_ref[2:3, :])


def _post_block(x, o_a, o_b, o_c, sg, mod, norms, wa, wb, wc, wo, w1, w2):
    bsz, n_tok, _ = x.shape
    tm = ROW_TILE
    row = lambda w: pl.BlockSpec((1, tm, w), lambda b, i: (b, i, 0))
    return pl.pallas_call(
        _post_kernel,
        grid=(bsz, n_tok // tm),
        in_specs=[row(D_MODEL), row(Q_WIDTH), row(Q_WIDTH), row(Q_WIDTH), row(3 * D_MODEL),
                  pl.BlockSpec((1, 6, D_MODEL), lambda b, i: (b, 0, 0)),
                  _const_spec((8, D_MODEL)),
                  _const_spec((Q_WIDTH, D_MODEL)), _const_spec((Q_WIDTH, D_MODEL)),
                  _const_spec((Q_WIDTH, D_MODEL)), _const_spec((D_MODEL, D_MODEL)),
                  _const_spec((D_MODEL, MLP_HIDDEN)), _const_spec((MLP_HIDDEN, D_MODEL))],
        out_specs=row(D_MODEL),
        out_shape=jax.ShapeDtypeStruct((bsz, n_tok, D_MODEL), F32),
        compiler_params=_params(("arbitrary", "arbitrary")),
        name="post_mlp",
    )(x, o_a, o_b, o_c, sg, mod, norms, wa, wb, wc, wo, w1, w2)


def _rope_tables(n_tok):
    pos = jnp.arange(n_tok)
    row = (pos // GRID_W).astype(F32)
    col = (pos % GRID_W).astype(F32)
    n_freq = HEAD_DIM // 4
    freqs = ROPE_THETA ** (-jnp.arange(n_freq, dtype=F32) / n_freq)
    ang = jnp.concatenate([row[:, None] * freqs, col[:, None] * freqs], axis=-1)
    cos = jnp.repeat(jnp.cos(ang), 2, axis=-1)
    sign = jnp.tile(jnp.array([-1.0, 1.0], F32), HEAD_DIM // 2)
    sin = jnp.repeat(jnp.sin(ang), 2, axis=-1) * sign
    return jnp.tile(cos, (1, LANES // HEAD_DIM)), jnp.tile(sin, (1, LANES // HEAD_DIM))


def _extended_w_in(w_in_l):
    sizes = (Q_WIDTH, KV_WIDTH, KV_WIDTH, Q_WIDTH, KV_WIDTH, KV_WIDTH, Q_WIDTH, Q_WIDTH, Q_WIDTH)
    parts, start = [], 0
    for n in sizes:
        parts.append(w_in_l[:, start:start + n])
        start += n
    qa, ka, va, qb, kb, vb, qc, kc, vc = parts
    gates = w_in_l[:, start:]
    swap = lambda w: w[:, jnp.arange(w.shape[1]) ^ 1]
    cols = [qa * SM_SCALE, ka, swap(qa) * SM_SCALE, swap(ka), va,
            qb, kb, swap(qb), swap(kb), vb,
            qc * SM_SCALE, kc, vc, gates]
    return jnp.concatenate(cols, axis=1).astype(BF16)


def kernel(x, c, ctx, c_ctx, w_ada, b_ada, norm_mix_pre, norm_mix_post, w_in, sink_a, qnorm_b,
           knorm_b, rpb_c, w_br_a, w_br_b, w_br_c, w_out, norm_mlp_pre, norm_mlp_post,
           w_mlp_in, w_mlp_out):
    bsz, seq, _ = x.shape
    n_ctx = ctx.shape[1]
    cvec = jnp.concatenate([c, c_ctx[None, :], jnp.zeros((8 - bsz - 1, D_MODEL), F32)], axis=0)
    mod_all = _ada_modulation(cvec, w_ada, b_ada)

    cos_lat, sin_lat = _rope_tables(seq)
    cos_ctx = jnp.ones((n_ctx, LANES), F32)
    sin_ctx = jnp.zeros((n_ctx, LANES), F32)
    head_id = jnp.arange(QK_W) // HEAD_DIM
    gmat = (head_id[:, None] == head_id[None, :]).astype(BF16)
    tile2 = lambda g: jnp.tile(g, LANES // HEAD_DIM)
    swap1 = lambda g: g[jnp.arange(HEAD_DIM) ^ 1]

    x_lat, x_ctx = x, ctx
    for l in range(DEPTH):
        last = l == DEPTH - 1
        mod_lat = mod_all[l, :bsz].reshape(bsz, 6, D_MODEL)
        mod_ctx = jnp.broadcast_to(mod_all[l, bsz].reshape(1, 6, D_MODEL), (bsz, 6, D_MODEL))
        gq, gk = qnorm_b[l], knorm_b[l]
        vecs = jnp.stack([tile2(gq) * SM_SCALE, tile2(swap1(gq)) * SM_SCALE,
                          tile2(gk), tile2(swap1(gk))]
                         + [jnp.zeros((LANES,), F32)] * 4)
        gpre = norm_mix_pre[l][None, :]
        w_ext = _extended_w_in(w_in[l])
        norms = jnp.stack([norm_mix_post[l], norm_mlp_pre[l], norm_mlp_post[l]]
                          + [jnp.zeros((D_MODEL,), F32)] * 5)
        wa, wb, wc = (w.astype(BF16) for w in (w_br_a[l], w_br_b[l], w_br_c[l]))
        wo = w_out[l].astype(BF16)
        w1 = w_mlp_in[l].astype(BF16)
        w2 = w_mlp_out[l].astype(BF16)
        tabs = _neighbourhood_tables(rpb_c[l])
        sink = sink_a[l]

        proj_lat = _input_projection(x_lat, mod_lat, gpre, vecs, cos_lat, sin_lat, gmat, w_ext)
        proj_ctx = _input_projection(x_ctx, mod_ctx, gpre, vecs, cos_ctx, sin_ctx, gmat, w_ext)
        qa, ka, va, qb, kb, vb, qc, kc, vc, sg = proj_lat
        qa_c, ka_c, va_c, qb_c, kb_c, vb_c, qc_c, kc_c, vc_c, sg_c = proj_ctx

        o_a = _attention_a(sink, qa, ka, va, ka_c, va_c)
        o_b = _attention_b(qb, kb, vb, kb_c, vb_c)
        o_c = _attention_c(tabs, qc, kc, vc, kc_c, vc_c)
        x_lat = _post_block(x_lat, o_a, o_b, o_c, sg, mod_lat, norms, wa, wb, wc, wo, w1, w2)
        if not last:
            o_a_c, o_b_c, o_c_c = _attention_ctx(sink, qa_c, ka_c, va_c, qb_c, kb_c, vb_c,
                                                 qc_c, kc_c, vc_c)
            x_ctx = _post_block(x_ctx, o_a_c, o_b_c, o_c_c, sg_c, mod_ctx, norms,
                                wa, wb, wc, wo, w1, w2)
    return x_lat
```

```python
import jax
import jax.numpy as jnp
import numpy as np
from jax import lax
from jax.experimental import pallas as pl
from jax.experimental.pallas import tpu as pltpu

D_MODEL = 1024
SEQ = 8192
DEPTH = 2
GRID_W = 64
CTX_LEN = 256
HEAD_DIM = 64
A_Q_HEADS = 6
A_KV_HEADS = 2
WINDOW = 128
B_Q_HEADS = 6
B_KV_HEADS = 2
C_HEADS = 6
NA_KH_MAX = 8
NA_KW = 16
Q_WIDTH = 6 * HEAD_DIM
KV_WIDTH = 2 * HEAD_DIM
MLP_HIDDEN = 4 * D_MODEL
ROPE_THETA = 10000.0
NORM_EPS = 1e-6
NEG_INF = -1e30
SM_SCALE = HEAD_DIM ** -0.5
LOG2_E = 1.4426950408889634

V7X_VMEM_LIMIT_BYTES = 56 * 1024 * 1024
LANES = 128
ROW_TILE = 256
KV_CHUNK = 512
ADA_COLS = 1536

BF16 = jnp.bfloat16
F32 = jnp.float32

A_OFF = 0
B_OFF = A_OFF + 1152
C_OFF = B_OFF + 1152
G_OFF = C_OFF + 1152
EXT_COLS = G_OFF + 3 * D_MODEL
QK_W = Q_WIDTH + KV_WIDTH


def _params(semantics):
    return pltpu.CompilerParams(dimension_semantics=semantics,
                                vmem_limit_bytes=V7X_VMEM_LIMIT_BYTES)


def _const_spec(shape):
    nd = len(shape)
    return pl.BlockSpec(shape, lambda *_: (0,) * nd, pipeline_mode=pl.Buffered(1))


def _dot(a, b):
    return jnp.dot(a, b, preferred_element_type=F32)


def _dot_nt(a, b):
    return lax.dot_general(a, b, (((1,), (1,)), ((), ())), preferred_element_type=F32)


def _rms(x):
    return x * lax.rsqrt(jnp.mean(x * x, axis=-1, keepdims=True) + NORM_EPS)


def _ada_kernel(c_ref, w_ref, b_ref, o_ref):
    c = c_ref[...]
    a = c / (1.0 + jnp.exp(-c))
    w = w_ref[0]
    a_hi = a.astype(BF16)
    a_lo = (a - a_hi.astype(F32)).astype(BF16)
    w_hi = w.astype(BF16)
    w_lo = (w - w_hi.astype(F32)).astype(BF16)
    acc = _dot(a_hi, w_hi) + _dot(a_hi, w_lo) + _dot(a_lo, w_hi)
    o_ref[0] = acc + b_ref[0]


def _ada_modulation(cvec, w_ada, b_ada):
    n_col = 6 * D_MODEL
    return pl.pallas_call(
        _ada_kernel,
        grid=(DEPTH, n_col // ADA_COLS),
        in_specs=[
            pl.BlockSpec((8, D_MODEL), lambda l, j: (0, 0)),
            pl.BlockSpec((1, D_MODEL, ADA_COLS), lambda l, j: (l, 0, j)),
            pl.BlockSpec((1, 1, ADA_COLS), lambda l, j: (l, 0, j)),
        ],
        out_specs=pl.BlockSpec((1, 8, ADA_COLS), lambda l, j: (l, 0, j)),
        out_shape=jax.ShapeDtypeStruct((DEPTH, 8, n_col), F32),
        compiler_params=_params(("arbitrary", "arbitrary")),
        name="ada_mod",
    )(cvec, w_ada, b_ada.reshape(DEPTH, 1, n_col))


def _proj_kernel(x_ref, mod_ref, gpre_ref, vecs_ref, cos_ref, sin_ref, gmat_ref, w_ref,
                 qa_ref, ka_ref, va_ref, qb_ref, kb_ref, vb_ref, qc_ref, kc_ref, vc_ref, sg_ref):
    x = x_ref[0]
    h = _rms(x) * gpre_ref[...]
    h = h * (1.0 + mod_ref[0, 1:2, :]) + mod_ref[0, 0:1, :]
    hb = h.astype(BF16)
    cos = cos_ref[...]
    sin = sin_ref[...]

    ra = _dot(hb, w_ref[:, A_OFF:A_OFF + 1152])
    for j in range(4):
        lo = j * LANES
        roped = ra[:, lo:lo + LANES] * cos + ra[:, QK_W + lo:QK_W + lo + LANES] * sin
        if j < 3:
            qa_ref[0, :, lo:lo + LANES] = roped.astype(BF16)
        else:
            ka_ref[0] = roped.astype(BF16)
    va_ref[0] = ra[:, 2 * QK_W:2 * QK_W + KV_WIDTH].astype(BF16)

    rb = _dot(hb, w_ref[:, B_OFF:B_OFF + 1152])
    xb = rb[:, 0:QK_W]
    sq = xb * xb
    sq_hi = sq.astype(BF16)
    sq_lo = (sq - sq_hi.astype(F32)).astype(BF16)
    head_ms = (_dot(sq_hi, gmat_ref[...]) + _dot(sq_lo, gmat_ref[...])) * (1.0 / HEAD_DIM)
    rinv = lax.rsqrt(head_ms + NORM_EPS)
    cq, sq_t = cos * vecs_ref[0:1, :], sin * vecs_ref[1:2, :]
    ck, sk_t = cos * vecs_ref[2:3, :], sin * vecs_ref[3:4, :]
    for j in range(4):
        lo = j * LANES
        c_t, s_t = (cq, sq_t) if j < 3 else (ck, sk_t)
        roped = rinv[:, lo:lo + LANES] * (rb[:, lo:lo + LANES] * c_t
                                          + rb[:, QK_W + lo:QK_W + lo + LANES] * s_t)
        if j < 3:
            qb_ref[0, lo:lo + LANES, :] = roped.T.astype(BF16)
        else:
            kb_ref[0, 0] = roped[:, 0:HEAD_DIM].astype(BF16)
            kb_ref[0, 1] = roped[:, HEAD_DIM:2 * HEAD_DIM].astype(BF16)
    vbt = rb[:, 2 * QK_W:2 * QK_W + KV_WIDTH].T.astype(BF16)
    pad_row = lax.broadcasted_iota(jnp.int32, (HEAD_DIM, x.shape[0]), 0)
    ones_pad = jnp.where(pad_row == 0, 1.0, 0.0).astype(BF16)
    for g in range(B_KV_HEADS):
        vb_ref[0, g, 0:HEAD_DIM, :] = vbt[g * HEAD_DIM:(g + 1) * HEAD_DIM]
        vb_ref[0, g, HEAD_DIM:2 * HEAD_DIM, :] = ones_pad

    rc = _dot(hb, w_ref[:, C_OFF:C_OFF + 1152])
    qc_ref[0] = rc[:, 0:Q_WIDTH].astype(BF16)
    kc_ref[0] = rc[:, Q_WIDTH:2 * Q_WIDTH].astype(BF16)
    vc_ref[0] = rc[:, 2 * Q_WIDTH:3 * Q_WIDTH].astype(BF16)

    for j in range(3):
        lo = G_OFF + j * D_MODEL
        g = _dot(hb, w_ref[:, lo:lo + D_MODEL])
        sg_ref[0, :, j * D_MODEL:(j + 1) * D_MODEL] = (1.0 / (1.0 + jnp.exp(-g))).astype(BF16)


def _input_projection(x, mod, gpre, vecs, cos_t, sin_t, gmat, w_ext):
    bsz, n_tok, _ = x.shape
    tm = ROW_TILE
    row3 = lambda w: pl.BlockSpec((1, tm, w), lambda b, i: (b, i, 0))
    head4 = pl.BlockSpec((1, 2, tm, HEAD_DIM), lambda b, i: (b, 0, i, 0))
    bf = lambda *s: jax.ShapeDtypeStruct(s, BF16)
    return pl.pallas_call(
        _proj_kernel,
        grid=(bsz, n_tok // tm),
        in_specs=[
            row3(D_MODEL),
            pl.BlockSpec((1, 6, D_MODEL), lambda b, i: (b, 0, 0)),
            _const_spec((1, D_MODEL)),
            _const_spec((8, LANES)),
            pl.BlockSpec((tm, LANES), lambda b, i: (i, 0)),
            pl.BlockSpec((tm, LANES), lambda b, i: (i, 0)),
            _const_spec((QK_W, QK_W)),
            _const_spec((D_MODEL, EXT_COLS)),
        ],
        out_specs=[row3(Q_WIDTH), row3(KV_WIDTH), row3(KV_WIDTH),
                   pl.BlockSpec((1, Q_WIDTH, tm), lambda b, i: (b, 0, i)), head4,
                   pl.BlockSpec((1, 2, 2 * HEAD_DIM, tm), lambda b, i: (b, 0, 0, i)),
                   row3(Q_WIDTH), row3(Q_WIDTH), row3(Q_WIDTH),
                   row3(3 * D_MODEL)],
        out_shape=[bf(bsz, n_tok, Q_WIDTH), bf(bsz, n_tok, KV_WIDTH), bf(bsz, n_tok, KV_WIDTH),
                   bf(bsz, Q_WIDTH, n_tok), bf(bsz, 2, n_tok, HEAD_DIM), bf(bsz, 2, 2 * HEAD_DIM, n_tok),
                   bf(bsz, n_tok, Q_WIDTH), bf(bsz, n_tok, Q_WIDTH), bf(bsz, n_tok, Q_WIDTH),
                   bf(bsz, n_tok, 3 * D_MODEL)],
        compiler_params=_params(("arbitrary", "arbitrary")),
        name="in_proj",
    )(x, mod, gpre, vecs, cos_t, sin_t, gmat, w_ext)


def _joint_softmax_pv(scores, values, sink=None):
    m = None
    for s in scores:
        pm = jnp.max(s, axis=-1, keepdims=True)
        m = pm if m is None else jnp.maximum(m, pm)
    if sink is not None:
        m = jnp.maximum(m, sink)
    denom = None
    out = None
    for s, v in zip(scores, values):
        p = jnp.exp(s - m)
        ps = jnp.sum(p, axis=-1, keepdims=True)
        denom = ps if denom is None else denom + ps
        pv = _dot(p.astype(BF16), v)
        out = pv if out is None else out + pv
    if sink is not None:
        denom = denom + jnp.exp(sink - m)
    return out / denom


def _attn_a_kernel(sink_ref, q_ref, k0_ref, k1_ref, k2_ref, v0_ref, v1_ref, v2_ref,
                   kc_ref, vc_ref, o_ref):
    t = pl.program_id(1)
    nt = pl.num_programs(1)
    tq = ROW_TILE
    qi = lax.broadcasted_iota(jnp.int32, (tq, 3 * tq), 0)
    kj = lax.broadcasted_iota(jnp.int32, (tq, 3 * tq), 1)
    rel = kj - tq - qi
    valid = (rel <= WINDOW) & (rel >= -WINDOW)
    valid = valid & ((kj >= tq) | (t > 0)) & ((kj < 2 * tq) | (t < nt - 1))
    mask = jnp.where(valid, 0.0, NEG_INF).astype(F32)

    q = q_ref[0]
    kwin = jnp.concatenate([k0_ref[0], k1_ref[0], k2_ref[0]], axis=0)
    vwin = jnp.concatenate([v0_ref[0], v1_ref[0], v2_ref[0]], axis=0)
    kctx = kc_ref[0]
    vctx = vc_ref[0]
    group = A_Q_HEADS // A_KV_HEADS
    for h in range(A_Q_HEADS):
        g = h // group
        hs = slice(h * HEAD_DIM, (h + 1) * HEAD_DIM)
        gs = slice(g * HEAD_DIM, (g + 1) * HEAD_DIM)
        qh = q[:, hs]
        s_win = _dot_nt(qh, kwin[:, gs]) + mask
        s_ctx = _dot_nt(qh, kctx[:, gs])
        o = _joint_softmax_pv([s_win, s_ctx], [vwin[:, gs], vctx[:, gs]], sink=sink_ref[h])
        o_ref[0, :, hs] = o.astype(BF16)


def _attention_a(sink, q, k, v, k_ctx, v_ctx):
    bsz, n_tok, _ = q.shape
    tq = ROW_TILE
    nt = n_tok // tq
    prev = lambda b, i: (b, jnp.maximum(i - 1, 0), 0)
    cur = lambda b, i: (b, i, 0)
    nxt = lambda b, i: (b, jnp.minimum(i + 1, nt - 1), 0)
    kv = lambda im: pl.BlockSpec((1, tq, KV_WIDTH), im)
    ctx = pl.BlockSpec((1, CTX_LEN, KV_WIDTH), lambda b, i: (b, 0, 0))
    return pl.pallas_call(
        _attn_a_kernel,
        grid=(bsz, nt),
        in_specs=[pl.BlockSpec(memory_space=pltpu.SMEM),
                  pl.BlockSpec((1, tq, Q_WIDTH), cur),
                  kv(prev), kv(cur), kv(nxt), kv(prev), kv(cur), kv(nxt), ctx, ctx],
        out_specs=pl.BlockSpec((1, tq, Q_WIDTH), cur),
        out_shape=jax.ShapeDtypeStruct((bsz, n_tok, Q_WIDTH), BF16),
        compiler_params=_params(("arbitrary", "arbitrary")),
        name="attn_window",
    )(sink, q, k, k, k, v, v, v, k_ctx, v_ctx)


def _flash_update_t(carry, k, vt, qt):
    m, acc = carry
    st = _dot(k, qt)
    m_new = jnp.maximum(m, jnp.max(st, axis=0, keepdims=True))
    alpha = jnp.exp2(m - m_new)
    p = jnp.exp2(st - m_new).astype(BF16)
    return m_new, alpha * acc + _dot(vt, p)


def _flash_init_t(n_q):
    return jnp.full((1, n_q), NEG_INF, F32), jnp.zeros((2 * HEAD_DIM, n_q), F32)


def _flash_finish_t(carry):
    _, acc = carry
    return acc[0:HEAD_DIM] / acc[HEAD_DIM:HEAD_DIM + 1]


def _attn_b_kernel(qt_ref, kc_ref, vtc_ref, k_ref, vt_ref, o_ref):
    tq = o_ref.shape[1]
    n_chunks = k_ref.shape[2] // KV_CHUNK
    group = B_Q_HEADS // B_KV_HEADS
    qts = [jnp.concatenate(
        [qt_ref[0, (g * group + j) * HEAD_DIM:(g * group + j + 1) * HEAD_DIM, :]
         for j in range(group)], axis=1) for g in range(B_KV_HEADS)]
    carries = tuple(_flash_update_t(_flash_init_t(group * tq), kc_ref[0, g], vtc_ref[0, g], qts[g])
                    for g in range(B_KV_HEADS))

    def body(c, carries):
        start = pl.multiple_of(c * KV_CHUNK, KV_CHUNK)
        return tuple(_flash_update_t(carries[g], k_ref[0, g, pl.ds(start, KV_CHUNK), :],
                                     vt_ref[0, g, :, pl.ds(start, KV_CHUNK)], qts[g])
                     for g in range(B_KV_HEADS))

    carries = lax.fori_loop(0, n_chunks, body, carries)
    heads_t = []
    for g in range(B_KV_HEADS):
        og = _flash_finish_t(carries[g])
        heads_t += [og[:, j * tq:(j + 1) * tq] for j in range(group)]
    o_ref[0] = jnp.concatenate(heads_t, axis=0).T.astype(BF16)


def _attention_b(qt, k, vt, k_ctx, vt_ctx):
    bsz, _, n_tok = qt.shape
    tq = ROW_TILE
    keys = lambda n: pl.BlockSpec((1, 2, n, HEAD_DIM), lambda b, i: (b, 0, 0, 0))
    vals = lambda n: pl.BlockSpec((1, 2, 2 * HEAD_DIM, n), lambda b, i: (b, 0, 0, 0))
    return pl.pallas_call(
        _attn_b_kernel,
        grid=(bsz, n_tok // tq),
        in_specs=[pl.BlockSpec((1, Q_WIDTH, tq), lambda b, i: (b, 0, i)),
                  keys(CTX_LEN), vals(CTX_LEN), keys(n_tok), vals(n_tok)],
        out_specs=pl.BlockSpec((1, tq, Q_WIDTH), lambda b, i: (b, i, 0)),
        out_shape=jax.ShapeDtypeStruct((bsz, n_tok, Q_WIDTH), BF16),
        compiler_params=_params(("arbitrary", "arbitrary")),
        name="attn_global",
    )(qt, k_ctx, vt_ctx, k, vt)


def _attn_c_kernel(tab_ref, q_ref, k0_ref, k1_ref, k2_ref, v0_ref, v1_ref, v2_ref,
                   kc_ref, vc_ref, o_ref):
    q = q_ref[0]
    kwin = jnp.concatenate([k0_ref[0], k1_ref[0], k2_ref[0]], axis=0)
    vwin = jnp.concatenate([v0_ref[0], v1_ref[0], v2_ref[0]], axis=0)
    kctx = kc_ref[0]
    vctx = vc_ref[0]
    for h in range(C_HEADS):
        hs = slice(h * HEAD_DIM, (h + 1) * HEAD_DIM)
        qh = q[:, hs]
        s_win = _dot_nt(qh, kwin[:, hs]) + tab_ref[0, h]
        s_ctx = _dot_nt(qh, kctx[:, hs])
        o = _joint_softmax_pv([s_win, s_ctx], [vwin[:, hs], vctx[:, hs]])
        o_ref[0, :, hs] = o.astype(BF16)


def _attention_c(tab, q, k, v, k_ctx, v_ctx):
    bsz, n_tok, _ = q.shape
    tq = ROW_TILE
    nt = n_tok // tq
    prev = lambda b, i: (b, jnp.maximum(i - 1, 0), 0)
    cur = lambda b, i: (b, i, 0)
    nxt = lambda b, i: (b, jnp.minimum(i + 1, nt - 1), 0)
    kv = lambda im: pl.BlockSpec((1, tq, Q_WIDTH), im)
    ctx = pl.BlockSpec((1, CTX_LEN, Q_WIDTH), lambda b, i: (b, 0, 0))
    variant = lambda b, i: (jnp.where(i == 0, 0, jnp.where(i == nt - 1, 2, 1)), 0, 0, 0)
    return pl.pallas_call(
        _attn_c_kernel,
        grid=(bsz, nt),
        in_specs=[pl.BlockSpec((1, C_HEADS, tq, 3 * tq), variant),
                  pl.BlockSpec((1, tq, Q_WIDTH), cur),
                  kv(prev), kv(cur), kv(nxt), kv(prev), kv(cur), kv(nxt), ctx, ctx],
        out_specs=pl.BlockSpec((1, tq, Q_WIDTH), cur),
        out_shape=jax.ShapeDtypeStruct((bsz, n_tok, Q_WIDTH), BF16),
        compiler_params=_params(("arbitrary", "arbitrary")),
        name="attn_nbr",
    )(tab, q, k, k, k, v, v, v, k_ctx, v_ctx)


def _na_table_kernel(rowmat_ref, o_ref):
    rows = SEQ // GRID_W
    rpt = ROW_TILE // GRID_W
    nt = rows // rpt
    kh = min(NA_KH_MAX, rows)
    masked = jnp.full((GRID_W, GRID_W), NEG_INF, F32)
    for v, t in enumerate((0, 1, nt - 1)):
        for rl in range(rpt):
            r = rpt * t + rl
            rs = min(max(r - kh // 2, 0), rows - kh)
            for sk in range(3 * rpt):
                kr = rpt * (t - 1) + sk
                blk = rowmat_ref[0, kr - r + NA_KH_MAX - 1] if rs <= kr < rs + kh else masked
                o_ref[v, 0, rl * GRID_W:(rl + 1) * GRID_W, sk * GRID_W:(sk + 1) * GRID_W] = blk


def _neighbourhood_tables(rpb):
    c = np.arange(GRID_W)
    ws = np.clip(c - NA_KW // 2, 0, GRID_W - NA_KW)
    valid_c = (c[None, :] >= ws[:, None]) & (c[None, :] < ws[:, None] + NA_KW)
    coff = np.clip(c[None, :] - c[:, None] + NA_KW - 1, 0, 2 * NA_KW - 2)
    onehot = (coff[:, :, None] == np.arange(2 * NA_KW - 1)).astype(np.float32)
    rowmat = jnp.einsum("hij,ckj->hick", rpb, onehot, precision=lax.Precision.HIGHEST)
    rowmat = jnp.where(valid_c, rowmat, NEG_INF).astype(F32)
    n_off = 2 * NA_KH_MAX - 1
    return pl.pallas_call(
        _na_table_kernel,
        grid=(C_HEADS,),
        in_specs=[pl.BlockSpec((1, n_off, GRID_W, GRID_W), lambda h: (h, 0, 0, 0))],
        out_specs=pl.BlockSpec((3, 1, ROW_TILE, 3 * ROW_TILE), lambda h: (0, h, 0, 0)),
        out_shape=jax.ShapeDtypeStruct((3, C_HEADS, ROW_TILE, 3 * ROW_TILE), F32),
        compiler_params=_params(("arbitrary",)),
        name="na_tables",
    )(rowmat)


def _attn_ctx_kernel(sink_ref, qa_ref, ka_ref, va_ref, qb_ref, kb_ref, vb_ref,
                     qc_ref, kc_ref, vc_ref, oa_ref, ob_ref, oc_ref):
    group = A_Q_HEADS // A_KV_HEADS
    qa, ka, va = qa_ref[0], ka_ref[0], va_ref[0]
    qc, kc, vc = qc_ref[0], kc_ref[0], vc_ref[0]
    n_q = qa.shape[0]
    ob_t = []
    for h in range(A_Q_HEADS):
        g = h // group
        hs = slice(h * HEAD_DIM, (h + 1) * HEAD_DIM)
        gs = slice(g * HEAD_DIM, (g + 1) * HEAD_DIM)
        o = _joint_softmax_pv([_dot_nt(qa[:, hs], ka[:, gs])], [va[:, gs]], sink=sink_ref[h])
        oa_ref[0, :, hs] = o.astype(BF16)
        ob_t.append(_flash_finish_t(_flash_update_t(
            _flash_init_t(n_q), kb_ref[0, g], vb_ref[0, g], qb_ref[0, hs, :])))
        o = _joint_softmax_pv([_dot_nt(qc[:, hs], kc[:, hs])], [vc[:, hs]])
        oc_ref[0, :, hs] = o.astype(BF16)
    ob_ref[0] = jnp.concatenate(ob_t, axis=0).T.astype(BF16)


def _attention_ctx(sink, qa, ka, va, qb, kb, vb, qc, kc, vc):
    bsz = qa.shape[0]
    row = lambda w: pl.BlockSpec((1, CTX_LEN, w), lambda b: (b, 0, 0))
    head = pl.BlockSpec((1, 2, CTX_LEN, HEAD_DIM), lambda b: (b, 0, 0, 0))
    head_t = pl.BlockSpec((1, 2, 2 * HEAD_DIM, CTX_LEN), lambda b: (b, 0, 0, 0))
    out = jax.ShapeDtypeStruct((bsz, CTX_LEN, Q_WIDTH), BF16)
    return pl.pallas_call(
        _attn_ctx_kernel,
        grid=(bsz,),
        in_specs=[pl.BlockSpec(memory_space=pltpu.SMEM),
                  row(Q_WIDTH), row(KV_WIDTH), row(KV_WIDTH),
                  pl.BlockSpec((1, Q_WIDTH, CTX_LEN), lambda b: (b, 0, 0)), head, head_t,
                  row(Q_WIDTH), row(Q_WIDTH), row(Q_WIDTH)],
        out_specs=[row(Q_WIDTH)] * 3,
        out_shape=[out] * 3,
        compiler_params=_params(("arbitrary",)),
        name="attn_ctx",
    )(sink, qa, ka, va, qb, kb, vb, qc, kc, vc)


def _post_kernel(x_ref, oa_ref, ob_ref, oc_ref, sg_ref, mod_ref, norms_ref,
                 wa_ref, wb_ref, wc_ref, wo_ref, w1_ref, w2_ref, o_ref):
    x = x_ref[0]
    merged = None
    for j, (o_r, w_r) in enumerate(((oa_ref, wa_ref), (ob_ref, wb_ref), (oc_ref, wc_ref))):
        gate = sg_ref[0, :, j * D_MODEL:(j + 1) * D_MODEL].astype(F32)
        term = gate * _dot(o_r[0], w_r[...])
        merged = term if merged is None else merged + term
    y = _dot(merged.astype(BF16), wo_ref[...])
    x1 = x + mod_ref[0, 2:3, :] * (_rms(y) * norms_ref[0:1, :])
    h2 = _rms(x1) * norms_ref[1:2, :]
    h2 = (h2 * (1.0 + mod_ref[0, 4:5, :]) + mod_ref[0, 3:4, :]).astype(BF16)
    z = None
    n_chunk = MLP_HIDDEN // D_MODEL
    for j in range(n_chunk):
        cs = slice(j * D_MODEL, (j + 1) * D_MODEL)
        a = jnp.maximum(_dot(h2, w1_ref[:, cs]), 0.0)
        zj = _dot((a * a).astype(BF16), w2_ref[cs, :])
        z = zj if z is None else z + zj
    o_ref[0] = x1 + mod_ref[0, 5:6, :] * (_rms(z) * norms_ref[2:3, :])


def _post_block(x, o_a, o_b, o_c, sg, mod, norms, wa, wb, wc, wo, w1, w2):
    bsz, n_tok, _ = x.shape
    tm = ROW_TILE
    row = lambda w: pl.BlockSpec((1, tm, w), lambda b, i: (b, i, 0))
    return pl.pallas_call(
        _post_kernel,
        grid=(bsz, n_tok // tm),
        in_specs=[row(D_MODEL), row(Q_WIDTH), row(Q_WIDTH), row(Q_WIDTH), row(3 * D_MODEL),
                  pl.BlockSpec((1, 6, D_MODEL), lambda b, i: (b, 0, 0)),
                  _const_spec((8, D_MODEL)),
                  _const_spec((Q_WIDTH, D_MODEL)), _const_spec((Q_WIDTH, D_MODEL)),
                  _const_spec((Q_WIDTH, D_MODEL)), _const_spec((D_MODEL, D_MODEL)),
                  _const_spec((D_MODEL, MLP_HIDDEN)), _const_spec((MLP_HIDDEN, D_MODEL))],
        out_specs=row(D_MODEL),
        out_shape=jax.ShapeDtypeStruct((bsz, n_tok, D_MODEL), F32),
        compiler_params=_params(("arbitrary", "arbitrary")),
        name="post_mlp",
    )(x, o_a, o_b, o_c, sg, mod, norms, wa, wb, wc, wo, w1, w2)


def _rope_tables(n_tok):
    pos = jnp.arange(n_tok)
    row = (pos // GRID_W).astype(F32)
    col = (pos % GRID_W).astype(F32)
    n_freq = HEAD_DIM // 4
    freqs = ROPE_THETA ** (-jnp.arange(n_freq, dtype=F32) / n_freq)
    ang = jnp.concatenate([row[:, None] * freqs, col[:, None] * freqs], axis=-1)
    cos = jnp.repeat(jnp.cos(ang), 2, axis=-1)
    sign = jnp.tile(jnp.array([-1.0, 1.0], F32), HEAD_DIM // 2)
    sin = jnp.repeat(jnp.sin(ang), 2, axis=-1) * sign
    return jnp.tile(cos, (1, LANES // HEAD_DIM)), jnp.tile(sin, (1, LANES // HEAD_DIM))


def _extended_w_in(w_in_l):
    sizes = (Q_WIDTH, KV_WIDTH, KV_WIDTH, Q_WIDTH, KV_WIDTH, KV_WIDTH, Q_WIDTH, Q_WIDTH, Q_WIDTH)
    parts, start = [], 0
    for n in sizes:
        parts.append(w_in_l[:, start:start + n])
        start += n
    qa, ka, va, qb, kb, vb, qc, kc, vc = parts
    gates = w_in_l[:, start:]
    swap = lambda w: w[:, jnp.arange(w.shape[1]) ^ 1]
    cols = [qa * SM_SCALE, ka, swap(qa) * SM_SCALE, swap(ka), va,
            qb, kb, swap(qb), swap(kb), vb,
            qc * SM_SCALE, kc, vc, gates]
    return jnp.concatenate(cols, axis=1).astype(BF16)


def kernel(x, c, ctx, c_ctx, w_ada, b_ada, norm_mix_pre, norm_mix_post, w_in, sink_a, qnorm_b,
           knorm_b, rpb_c, w_br_a, w_br_b, w_br_c, w_out, norm_mlp_pre, norm_mlp_post,
           w_mlp_in, w_mlp_out):
    bsz, seq, _ = x.shape
    n_ctx = ctx.shape[1]
    cvec = jnp.concatenate([c, c_ctx[None, :], jnp.zeros((8 - bsz - 1, D_MODEL), F32)], axis=0)
    mod_all = _ada_modulation(cvec, w_ada, b_ada)

    cos_lat, sin_lat = _rope_tables(seq)
    cos_ctx = jnp.ones((n_ctx, LANES), F32)
    sin_ctx = jnp.zeros((n_ctx, LANES), F32)
    head_id = jnp.arange(QK_W) // HEAD_DIM
    gmat = (head_id[:, None] == head_id[None, :]).astype(BF16)
    tile2 = lambda g: jnp.tile(g, LANES // HEAD_DIM)
    swap1 = lambda g: g[jnp.arange(HEAD_DIM) ^ 1]

    x_lat, x_ctx = x, ctx
    for l in range(DEPTH):
        last = l == DEPTH - 1
        mod_lat = mod_all[l, :bsz].reshape(bsz, 6, D_MODEL)
        mod_ctx = jnp.broadcast_to(mod_all[l, bsz].reshape(1, 6, D_MODEL), (bsz, 6, D_MODEL))
        gq, gk = qnorm_b[l], knorm_b[l]
        vecs = jnp.stack([tile2(gq) * (SM_SCALE * LOG2_E), tile2(swap1(gq)) * (SM_SCALE * LOG2_E),
                          tile2(gk), tile2(swap1(gk))]
                         + [jnp.zeros((LANES,), F32)] * 4)
        gpre = norm_mix_pre[l][None, :]
        w_ext = _extended_w_in(w_in[l])
        norms = jnp.stack([norm_mix_post[l], norm_mlp_pre[l], norm_mlp_post[l]]
                          + [jnp.zeros((D_MODEL,), F32)] * 5)
        wa, wb, wc = (w.astype(BF16) for w in (w_br_a[l], w_br_b[l], w_br_c[l]))
        wo = w_out[l].astype(BF16)
        w1 = w_mlp_in[l].astype(BF16)
        w2 = w_mlp_out[l].astype(BF16)
        tabs = _neighbourhood_tables(rpb_c[l])
        sink = sink_a[l]

        proj_lat = _input_projection(x_lat, mod_lat, gpre, vecs, cos_lat, sin_lat, gmat, w_ext)
        proj_ctx = _input_projection(x_ctx, mod_ctx, gpre, vecs, cos_ctx, sin_ctx, gmat, w_ext)
        qa, ka, va, qb, kb, vb, qc, kc, vc, sg = proj_lat
        qa_c, ka_c, va_c, qb_c, kb_c, vb_c, qc_c, kc_c, vc_c, sg_c = proj_ctx

        o_a = _attention_a(sink, qa, ka, va, ka_c, va_c)
        o_b = _attention_b(qb, kb, vb, kb_c, vb_c)
        o_c = _attention_c(tabs, qc, kc, vc, kc_c, vc_c)
        x_lat = _post_block(x_lat, o_a, o_b, o_c, sg, mod_lat, norms, wa, wb, wc, wo, w1, w2)
        if not last:
            o_a_c, o_b_c, o_c_c = _attention_ctx(sink, qa_c, ka_c, va_c, qb_c, kb_c, vb_c,
                                                 qc_c, kc_c, vc_c)
            x_ctx = _post_block(x_ctx, o_a_c, o_b_c, o_c_c, sg_c, mod_ctx, norms,
                                wa, wb, wc, wo, w1, w2)
    return x_lat
```

```python
import jax
import jax.numpy as jnp
import numpy as np
from jax import lax
from jax.experimental import pallas as pl
from jax.experimental.pallas import tpu as pltpu

D_MODEL = 1024
SEQ = 8192
DEPTH = 2
GRID_W = 64
CTX_LEN = 256
HEAD_DIM = 64
A_Q_HEADS = 6
A_KV_HEADS = 2
WINDOW = 128
B_Q_HEADS = 6
B_KV_HEADS = 2
C_HEADS = 6
NA_KH_MAX = 8
NA_KW = 16
Q_WIDTH = 6 * HEAD_DIM
KV_WIDTH = 2 * HEAD_DIM
MLP_HIDDEN = 4 * D_MODEL
ROPE_THETA = 10000.0
NORM_EPS = 1e-6
NEG_INF = -1e30
SM_SCALE = HEAD_DIM ** -0.5
LOG2_E = 1.4426950408889634
Q_SCALE = SM_SCALE * LOG2_E

V7X_VMEM_LIMIT_BYTES = 56 * 1024 * 1024
LANES = 128
ROW_TILE = 256
WIN_BLOCK = 128
KV_CHUNK = 512
ADA_COLS = 1536
MAX_SAFE_SCORE_BOUND = 60.0

BF16 = jnp.bfloat16
F32 = jnp.float32

A_OFF = 0
B_OFF = A_OFF + 1152
C_OFF = B_OFF + 1152
G_OFF = C_OFF + 1152
EXT_COLS = G_OFF + 3 * D_MODEL
QK_W = Q_WIDTH + KV_WIDTH


def _params(semantics):
    return pltpu.CompilerParams(dimension_semantics=semantics,
                                vmem_limit_bytes=V7X_VMEM_LIMIT_BYTES)


def _const_spec(shape):
    nd = len(shape)
    return pl.BlockSpec(shape, lambda *_: (0,) * nd, pipeline_mode=pl.Buffered(1))


def _dot(a, b):
    return jnp.dot(a, b, preferred_element_type=F32)


def _rms(x):
    return x * lax.rsqrt(jnp.mean(x * x, axis=-1, keepdims=True) + NORM_EPS)


def _ada_kernel(c_ref, w_ref, b_ref, o_ref):
    c = c_ref[...]
    a = c / (1.0 + jnp.exp(-c))
    w = w_ref[0]
    a_hi = a.astype(BF16)
    a_lo = (a - a_hi.astype(F32)).astype(BF16)
    w_hi = w.astype(BF16)
    w_lo = (w - w_hi.astype(F32)).astype(BF16)
    acc = _dot(a_hi, w_hi) + _dot(a_hi, w_lo) + _dot(a_lo, w_hi)
    o_ref[0] = acc + b_ref[0]


def _ada_modulation(cvec, w_ada, b_ada):
    n_col = 6 * D_MODEL
    return pl.pallas_call(
        _ada_kernel,
        grid=(DEPTH, n_col // ADA_COLS),
        in_specs=[
            pl.BlockSpec((8, D_MODEL), lambda l, j: (0, 0)),
            pl.BlockSpec((1, D_MODEL, ADA_COLS), lambda l, j: (l, 0, j)),
            pl.BlockSpec((1, 1, ADA_COLS), lambda l, j: (l, 0, j)),
        ],
        out_specs=pl.BlockSpec((1, 8, ADA_COLS), lambda l, j: (l, 0, j)),
        out_shape=jax.ShapeDtypeStruct((DEPTH, 8, n_col), F32),
        compiler_params=_params(("arbitrary", "arbitrary")),
        name="ada_mod",
    )(cvec, w_ada, b_ada.reshape(DEPTH, 1, n_col))


def _store_heads(k_ref, k, n_heads):
    for h in range(n_heads):
        k_ref[0, h] = k[:, h * HEAD_DIM:(h + 1) * HEAD_DIM].astype(BF16)


def _store_values_t(vt_ref, v, n_heads):
    vt = v.T.astype(BF16)
    pad_row = lax.broadcasted_iota(jnp.int32, (HEAD_DIM, v.shape[0]), 0)
    ones_pad = jnp.where(pad_row == 0, 1.0, 0.0).astype(BF16)
    for h in range(n_heads):
        vt_ref[0, h, 0:HEAD_DIM, :] = vt[h * HEAD_DIM:(h + 1) * HEAD_DIM]
        vt_ref[0, h, HEAD_DIM:2 * HEAD_DIM, :] = ones_pad


def _proj_kernel(x_ref, mod_ref, gpre_ref, vecs_ref, cos_ref, sin_ref, gmat_ref, w_ref,
                 qa_ref, ka_ref, va_ref, qb_ref, kb_ref, vb_ref, qc_ref, kc_ref, vc_ref, sg_ref):
    x = x_ref[0]
    h = _rms(x) * gpre_ref[...]
    h = h * (1.0 + mod_ref[0, 1:2, :]) + mod_ref[0, 0:1, :]
    hb = h.astype(BF16)
    cos = cos_ref[...]
    sin = sin_ref[...]

    ra = _dot(hb, w_ref[:, A_OFF:A_OFF + 1152])
    for j in range(4):
        lo = j * LANES
        roped = ra[:, lo:lo + LANES] * cos + ra[:, QK_W + lo:QK_W + lo + LANES] * sin
        if j < 3:
            qa_ref[0, lo:lo + LANES, :] = roped.T.astype(BF16)
        else:
            _store_heads(ka_ref, roped, A_KV_HEADS)
    _store_values_t(va_ref, ra[:, 2 * QK_W:2 * QK_W + KV_WIDTH], A_KV_HEADS)

    rb = _dot(hb, w_ref[:, B_OFF:B_OFF + 1152])
    xb = rb[:, 0:QK_W]
    sq = xb * xb
    sq_hi = sq.astype(BF16)
    sq_lo = (sq - sq_hi.astype(F32)).astype(BF16)
    head_ms = (_dot(sq_hi, gmat_ref[...]) + _dot(sq_lo, gmat_ref[...])) * (1.0 / HEAD_DIM)
    rinv = lax.rsqrt(head_ms + NORM_EPS)
    cq, sq_t = cos * vecs_ref[0:1, :], sin * vecs_ref[1:2, :]
    ck, sk_t = cos * vecs_ref[2:3, :], sin * vecs_ref[3:4, :]
    for j in range(4):
        lo = j * LANES
        c_t, s_t = (cq, sq_t) if j < 3 else (ck, sk_t)
        roped = rinv[:, lo:lo + LANES] * (rb[:, lo:lo + LANES] * c_t
                                          + rb[:, QK_W + lo:QK_W + lo + LANES] * s_t)
        if j < 3:
            qb_ref[0, lo:lo + LANES, :] = roped.T.astype(BF16)
        else:
            _store_heads(kb_ref, roped, B_KV_HEADS)
    _store_values_t(vb_ref, rb[:, 2 * QK_W:2 * QK_W + KV_WIDTH], B_KV_HEADS)

    rc = _dot(hb, w_ref[:, C_OFF:C_OFF + 1152])
    qc_ref[0] = rc[:, 0:Q_WIDTH].T.astype(BF16)
    _store_heads(kc_ref, rc[:, Q_WIDTH:2 * Q_WIDTH], C_HEADS)
    _store_values_t(vc_ref, rc[:, 2 * Q_WIDTH:3 * Q_WIDTH], C_HEADS)

    for j in range(3):
        lo = G_OFF + j * D_MODEL
        g = _dot(hb, w_ref[:, lo:lo + D_MODEL])
        sg_ref[0, :, j * D_MODEL:(j + 1) * D_MODEL] = (1.0 / (1.0 + jnp.exp(-g))).astype(BF16)


def _input_projection(x, mod, gpre, vecs, cos_t, sin_t, gmat, w_ext):
    bsz, n_tok, _ = x.shape
    tm = ROW_TILE
    bf = lambda *s: jax.ShapeDtypeStruct(s, BF16)
    q_spec = pl.BlockSpec((1, Q_WIDTH, tm), lambda b, i: (b, 0, i))
    k_spec = lambda n: pl.BlockSpec((1, n, tm, HEAD_DIM), lambda b, i: (b, 0, i, 0))
    v_spec = lambda n: pl.BlockSpec((1, n, 2 * HEAD_DIM, tm), lambda b, i: (b, 0, 0, i))
    qkv_specs, qkv_shapes = [], []
    for n in (A_KV_HEADS, B_KV_HEADS, C_HEADS):
        qkv_specs += [q_spec, k_spec(n), v_spec(n)]
        qkv_shapes += [bf(bsz, Q_WIDTH, n_tok), bf(bsz, n, n_tok, HEAD_DIM),
                       bf(bsz, n, 2 * HEAD_DIM, n_tok)]
    return pl.pallas_call(
        _proj_kernel,
        grid=(bsz, n_tok // tm),
        in_specs=[
            pl.BlockSpec((1, tm, D_MODEL), lambda b, i: (b, i, 0)),
            pl.BlockSpec((1, 6, D_MODEL), lambda b, i: (b, 0, 0)),
            _const_spec((1, D_MODEL)),
            _const_spec((8, LANES)),
            pl.BlockSpec((tm, LANES), lambda b, i: (i, 0)),
            pl.BlockSpec((tm, LANES), lambda b, i: (i, 0)),
            _const_spec((QK_W, QK_W)),
            _const_spec((D_MODEL, EXT_COLS)),
        ],
        out_specs=qkv_specs + [pl.BlockSpec((1, tm, 3 * D_MODEL), lambda b, i: (b, i, 0))],
        out_shape=qkv_shapes + [bf(bsz, n_tok, 3 * D_MODEL)],
        compiler_params=_params(("arbitrary", "arbitrary")),
        name="in_proj",
    )(x, mod, gpre, vecs, cos_t, sin_t, gmat, w_ext)


def _flash_update_t(carry, k, vt, qt, bias=None):
    m, acc = carry
    st = _dot(k, qt)
    if bias is not None:
        st = st + bias
    m_new = jnp.maximum(m, jnp.max(st, axis=0, keepdims=True))
    alpha = jnp.exp2(m - m_new)
    p = jnp.exp2(st - m_new).astype(BF16)
    return m_new, alpha * acc + _dot(vt, p)


def _bounded_update_t(acc, k, vt, qt, bound):
    p = jnp.exp2(_dot(k, qt) - bound).astype(BF16)
    return acc + _dot(vt, p)


def _flash_init_t(n_q):
    return jnp.full((1, n_q), NEG_INF, F32), jnp.zeros((2 * HEAD_DIM, n_q), F32)


def _flash_finish_t(carry, sink=None):
    m, acc = carry
    denom = acc[HEAD_DIM:HEAD_DIM + 1]
    if sink is not None:
        denom = denom + jnp.exp2(sink - m)
    return acc[0:HEAD_DIM] / denom


def _stack_heads_t(qt_ref, g, group):
    return jnp.concatenate(
        [qt_ref[0, (g * group + j) * HEAD_DIM:(g * group + j + 1) * HEAD_DIM, :]
         for j in range(group)], axis=1)


def _sink_row(sink_ref, g, group, n_q):
    return jnp.concatenate(
        [jnp.full((1, n_q), sink_ref[g * group + j], F32) for j in range(group)], axis=1)


def _attn_a_kernel(sink_ref, qt_ref, k0_ref, k1_ref, k2_ref, k3_ref, v0_ref, v1_ref, v2_ref, v3_ref,
                   kc_ref, vc_ref, o_ref):
    t = pl.program_id(1)
    nt = pl.num_programs(1)
    tq = o_ref.shape[1]
    n_win = 4 * WIN_BLOCK
    group = A_Q_HEADS // A_KV_HEADS
    kj = lax.broadcasted_iota(jnp.int32, (n_win, tq), 0)
    qi = lax.broadcasted_iota(jnp.int32, (n_win, tq), 1)
    rel = kj - WIN_BLOCK - qi
    valid = (rel <= WINDOW) & (rel >= -WINDOW)
    valid = valid & ((kj >= WIN_BLOCK) | (t > 0)) & ((kj < 3 * WIN_BLOCK) | (t < nt - 1))
    mask = jnp.where(valid, 0.0, NEG_INF).astype(F32)
    mask = jnp.concatenate([mask] * group, axis=1)
    heads_t = []
    for g in range(A_KV_HEADS):
        qt = _stack_heads_t(qt_ref, g, group)
        kwin = jnp.concatenate([r[0, g] for r in (k0_ref, k1_ref, k2_ref, k3_ref)], axis=0)
        vwin = jnp.concatenate([r[0, g] for r in (v0_ref, v1_ref, v2_ref, v3_ref)], axis=1)
        carry = _flash_update_t(_flash_init_t(group * tq), kwin, vwin, qt, bias=mask)
        carry = _flash_update_t(carry, kc_ref[0, g], vc_ref[0, g], qt)
        og = _flash_finish_t(carry, sink=_sink_row(sink_ref, g, group, tq))
        heads_t += [og[:, j * tq:(j + 1) * tq] for j in range(group)]
    o_ref[0] = jnp.concatenate(heads_t, axis=0).T.astype(BF16)


def _attention_a(sink, qt, k, vt, k_ctx, vt_ctx):
    bsz, _, n_tok = qt.shape
    tq = ROW_TILE
    nb = n_tok // WIN_BLOCK
    per = tq // WIN_BLOCK
    slot = lambda s: (lambda b, i: jnp.clip(per * i - 1 + s, 0, nb - 1))
    k_spec = lambda s: pl.BlockSpec((1, A_KV_HEADS, WIN_BLOCK, HEAD_DIM),
                                    lambda b, i: (b, 0, slot(s)(b, i), 0))
    v_spec = lambda s: pl.BlockSpec((1, A_KV_HEADS, 2 * HEAD_DIM, WIN_BLOCK),
                                    lambda b, i: (b, 0, 0, slot(s)(b, i)))
    return pl.pallas_call(
        _attn_a_kernel,
        grid=(bsz, n_tok // tq),
        in_specs=[pl.BlockSpec(memory_space=pltpu.SMEM),
                  pl.BlockSpec((1, Q_WIDTH, tq), lambda b, i: (b, 0, i))]
                 + [k_spec(s) for s in range(4)] + [v_spec(s) for s in range(4)]
                 + [pl.BlockSpec((1, A_KV_HEADS, CTX_LEN, HEAD_DIM), lambda b, i: (b, 0, 0, 0)),
                    pl.BlockSpec((1, A_KV_HEADS, 2 * HEAD_DIM, CTX_LEN), lambda b, i: (b, 0, 0, 0))],
        out_specs=pl.BlockSpec((1, tq, Q_WIDTH), lambda b, i: (b, i, 0)),
        out_shape=jax.ShapeDtypeStruct((bsz, n_tok, Q_WIDTH), BF16),
        compiler_params=_params(("arbitrary", "arbitrary")),
        name="attn_window",
    )(sink, qt, k, k, k, k, vt, vt, vt, vt, k_ctx, vt_ctx)


def _attn_b_kernel(bound_ref, qt_ref, kc_ref, vtc_ref, k_ref, vt_ref, o_ref):
    tq = o_ref.shape[1]
    n_chunks = k_ref.shape[2] // KV_CHUNK
    group = B_Q_HEADS // B_KV_HEADS
    groups = range(B_KV_HEADS)
    qts = [_stack_heads_t(qt_ref, g, group) for g in groups]

    def chunk(c, g):
        start = pl.multiple_of(c * KV_CHUNK, KV_CHUNK)
        return k_ref[0, g, pl.ds(start, KV_CHUNK), :], vt_ref[0, g, :, pl.ds(start, KV_CHUNK)]

    def write(outs_t):
        heads_t = [og[:, j * tq:(j + 1) * tq] for og in outs_t for j in range(group)]
        o_ref[0] = jnp.concatenate(heads_t, axis=0).T.astype(BF16)

    bound = bound_ref[0]
    bounded = bound <= MAX_SAFE_SCORE_BOUND

    @pl.when(bounded)
    def _():
        accs = tuple(_bounded_update_t(jnp.zeros((2 * HEAD_DIM, group * tq), F32),
                                       kc_ref[0, g], vtc_ref[0, g], qts[g], bound) for g in groups)
        accs = lax.fori_loop(
            0, n_chunks,
            lambda c, accs: tuple(_bounded_update_t(accs[g], *chunk(c, g), qts[g], bound)
                                  for g in groups), accs)
        write([acc[0:HEAD_DIM] / acc[HEAD_DIM:HEAD_DIM + 1] for acc in accs])

    @pl.when(jnp.logical_not(bounded))
    def _():
        carries = tuple(_flash_update_t(_flash_init_t(group * tq), kc_ref[0, g], vtc_ref[0, g], qts[g])
                        for g in groups)
        carries = lax.fori_loop(
            0, n_chunks,
            lambda c, carries: tuple(_flash_update_t(carries[g], *chunk(c, g), qts[g])
                                     for g in groups), carries)
        write([_flash_finish_t(carry) for carry in carries])


def _attention_b(bound, qt, k, vt, k_ctx, vt_ctx):
    bsz, _, n_tok = qt.shape
    tq = ROW_TILE
    keys = lambda n: pl.BlockSpec((1, B_KV_HEADS, n, HEAD_DIM), lambda b, i: (b, 0, 0, 0))
    vals = lambda n: pl.BlockSpec((1, B_KV_HEADS, 2 * HEAD_DIM, n), lambda b, i: (b, 0, 0, 0))
    return pl.pallas_call(
        _attn_b_kernel,
        grid=(bsz, n_tok // tq),
        in_specs=[pl.BlockSpec(memory_space=pltpu.SMEM),
                  pl.BlockSpec((1, Q_WIDTH, tq), lambda b, i: (b, 0, i)),
                  keys(CTX_LEN), vals(CTX_LEN), keys(n_tok), vals(n_tok)],
        out_specs=pl.BlockSpec((1, tq, Q_WIDTH), lambda b, i: (b, i, 0)),
        out_shape=jax.ShapeDtypeStruct((bsz, n_tok, Q_WIDTH), BF16),
        compiler_params=_params(("arbitrary", "arbitrary")),
        name="attn_global",
    )(bound, qt, k_ctx, vt_ctx, k, vt)


def _attn_c_kernel(tab_ref, qt_ref, k0_ref, k1_ref, k2_ref, v0_ref, v1_ref, v2_ref,
                   kc_ref, vc_ref, o_ref):
    tq = o_ref.shape[1]
    heads_t = []
    for h in range(C_HEADS):
        qt = qt_ref[0, h * HEAD_DIM:(h + 1) * HEAD_DIM, :]
        kwin = jnp.concatenate([r[0, h] for r in (k0_ref, k1_ref, k2_ref)], axis=0)
        vwin = jnp.concatenate([r[0, h] for r in (v0_ref, v1_ref, v2_ref)], axis=1)
        carry = _flash_update_t(_flash_init_t(tq), kwin, vwin, qt, bias=tab_ref[0, h])
        carry = _flash_update_t(carry, kc_ref[0, h], vc_ref[0, h], qt)
        heads_t.append(_flash_finish_t(carry))
    o_ref[0] = jnp.concatenate(heads_t, axis=0).T.astype(BF16)


def _attention_c(tab, qt, k, vt, k_ctx, vt_ctx):
    bsz, _, n_tok = qt.shape
    tq = ROW_TILE
    nt = n_tok // tq
    slot = lambda s: (lambda b, i: jnp.clip(i - 1 + s, 0, nt - 1))
    k_spec = lambda s: pl.BlockSpec((1, C_HEADS, tq, HEAD_DIM), lambda b, i: (b, 0, slot(s)(b, i), 0))
    v_spec = lambda s: pl.BlockSpec((1, C_HEADS, 2 * HEAD_DIM, tq),
                                    lambda b, i: (b, 0, 0, slot(s)(b, i)))
    variant = lambda b, i: (jnp.where(i == 0, 0, jnp.where(i == nt - 1, 2, 1)), 0, 0, 0)
    return pl.pallas_call(
        _attn_c_kernel,
        grid=(bsz, nt),
        in_specs=[pl.BlockSpec((1, C_HEADS, 3 * tq, tq), variant),
                  pl.BlockSpec((1, Q_WIDTH, tq), lambda b, i: (b, 0, i))]
                 + [k_spec(s) for s in range(3)] + [v_spec(s) for s in range(3)]
                 + [pl.BlockSpec((1, C_HEADS, CTX_LEN, HEAD_DIM), lambda b, i: (b, 0, 0, 0)),
                    pl.BlockSpec((1, C_HEADS, 2 * HEAD_DIM, CTX_LEN), lambda b, i: (b, 0, 0, 0))],
        out_specs=pl.BlockSpec((1, tq, Q_WIDTH), lambda b, i: (b, i, 0)),
        out_shape=jax.ShapeDtypeStruct((bsz, n_tok, Q_WIDTH), BF16),
        compiler_params=_params(("arbitrary", "arbitrary")),
        name="attn_nbr",
    )(tab, qt, k, k, k, vt, vt, vt, k_ctx, vt_ctx)


def _na_table_kernel(rowmat_ref, o_ref):
    rows = SEQ // GRID_W
    rpt = ROW_TILE // GRID_W
    nt = rows // rpt
    kh = min(NA_KH_MAX, rows)
    masked = jnp.full((GRID_W, GRID_W), NEG_INF, F32)
    for v, t in enumerate((0, 1, nt - 1)):
        for rl in range(rpt):
            r = rpt * t + rl
            rs = min(max(r - kh // 2, 0), rows - kh)
            for sk in range(3 * rpt):
                kr = rpt * (t - 1) + sk
                blk = rowmat_ref[0, kr - r + NA_KH_MAX - 1] if rs <= kr < rs + kh else masked
                o_ref[v, 0, sk * GRID_W:(sk + 1) * GRID_W, rl * GRID_W:(rl + 1) * GRID_W] = blk


def _neighbourhood_tables(rpb):
    c = np.arange(GRID_W)
    ws = np.clip(c - NA_KW // 2, 0, GRID_W - NA_KW)
    valid = (c[None, :] >= ws[:, None]) & (c[None, :] < ws[:, None] + NA_KW)
    coff = np.clip(c[None, :] - c[:, None] + NA_KW - 1, 0, 2 * NA_KW - 2)
    onehot = (coff[:, :, None] == np.arange(2 * NA_KW - 1)).astype(np.float32)
    rowmat = jnp.einsum("hij,ckj->hikc", rpb * LOG2_E, onehot, precision=lax.Precision.HIGHEST)
    rowmat = jnp.where(valid.T, rowmat, NEG_INF).astype(F32)
    n_off = 2 * NA_KH_MAX - 1
    return pl.pallas_call(
        _na_table_kernel,
        grid=(C_HEADS,),
        in_specs=[pl.BlockSpec((1, n_off, GRID_W, GRID_W), lambda h: (h, 0, 0, 0))],
        out_specs=pl.BlockSpec((3, 1, 3 * ROW_TILE, ROW_TILE), lambda h: (0, h, 0, 0)),
        out_shape=jax.ShapeDtypeStruct((3, C_HEADS, 3 * ROW_TILE, ROW_TILE), F32),
        compiler_params=_params(("arbitrary",)),
        name="na_tables",
    )(rowmat)


def _attn_ctx_kernel(sink_ref, qa_ref, ka_ref, va_ref, qb_ref, kb_ref, vb_ref,
                     qc_ref, kc_ref, vc_ref, oa_ref, ob_ref, oc_ref):
    group = A_Q_HEADS // A_KV_HEADS
    n_q = oa_ref.shape[1]
    outs = ([], [], [])
    for h in range(A_Q_HEADS):
        g = h // group
        hs = slice(h * HEAD_DIM, (h + 1) * HEAD_DIM)
        init = _flash_init_t(n_q)
        outs[0].append(_flash_finish_t(
            _flash_update_t(init, ka_ref[0, g], va_ref[0, g], qa_ref[0, hs, :]),
            sink=jnp.full((1, n_q), sink_ref[h], F32)))
        outs[1].append(_flash_finish_t(
            _flash_update_t(init, kb_ref[0, g], vb_ref[0, g], qb_ref[0, hs, :])))
        outs[2].append(_flash_finish_t(
            _flash_update_t(init, kc_ref[0, h], vc_ref[0, h], qc_ref[0, hs, :])))
    for o_ref, heads_t in zip((oa_ref, ob_ref, oc_ref), outs):
        o_ref[0] = jnp.concatenate(heads_t, axis=0).T.astype(BF16)


def _attention_ctx(sink, qa, ka, va, qb, kb, vb, qc, kc, vc):
    bsz = qa.shape[0]
    q_spec = pl.BlockSpec((1, Q_WIDTH, CTX_LEN), lambda b: (b, 0, 0))
    k_spec = lambda n: pl.BlockSpec((1, n, CTX_LEN, HEAD_DIM), lambda b: (b, 0, 0, 0))
    v_spec = lambda n: pl.BlockSpec((1, n, 2 * HEAD_DIM, CTX_LEN), lambda b: (b, 0, 0, 0))
    out = jax.ShapeDtypeStruct((bsz, CTX_LEN, Q_WIDTH), BF16)
    specs = [pl.BlockSpec(memory_space=pltpu.SMEM)]
    for n in (A_KV_HEADS, B_KV_HEADS, C_HEADS):
        specs += [q_spec, k_spec(n), v_spec(n)]
    return pl.pallas_call(
        _attn_ctx_kernel,
        grid=(bsz,),
        in_specs=specs,
        out_specs=[pl.BlockSpec((1, CTX_LEN, Q_WIDTH), lambda b: (b, 0, 0))] * 3,
        out_shape=[out] * 3,
        compiler_params=_params(("arbitrary",)),
        name="attn_ctx",
    )(sink, qa, ka, va, qb, kb, vb, qc, kc, vc)


def _post_kernel(x_ref, oa_ref, ob_ref, oc_ref, sg_ref, mod_ref, norms_ref,
                 wa_ref, wb_ref, wc_ref, wo_ref, w1_ref, w2_ref, o_ref):
    x = x_ref[0]
    merged = None
    for j, (o_r, w_r) in enumerate(((oa_ref, wa_ref), (ob_ref, wb_ref), (oc_ref, wc_ref))):
        gate = sg_ref[0, :, j * D_MODEL:(j + 1) * D_MODEL].astype(F32)
        term = gate * _dot(o_r[0], w_r[...])
        merged = term if merged is None else merged + term
    y = _dot(merged.astype(BF16), wo_ref[...])
    x1 = x + mod_ref[0, 2:3, :] * (_rms(y) * norms_ref[0:1, :])
    h2 = _rms(x1) * norms_ref[1:2, :]
    h2 = (h2 * (1.0 + mod_ref[0, 4:5, :]) + mod_ref[0, 3:4, :]).astype(BF16)
    z = None
    n_chunk = MLP_HIDDEN // D_MODEL
    for j in range(n_chunk):
        cs = slice(j * D_MODEL, (j + 1) * D_MODEL)
        a = jnp.maximum(_dot(h2, w1_ref[:, cs]), 0.0)
        zj = _dot((a * a).astype(BF16), w2_ref[cs, :])
        z = zj if z is None else z + zj
    o_ref[0] = x1 + mod_ref[0, 5:6, :] * (_rms(z) * norms_ref[2:3, :])


def _post_block(x, o_a, o_b, o_c, sg, mod, norms, wa, wb, wc, wo, w1, w2):
    bsz, n_tok, _ = x.shape
    tm = ROW_TILE
    row = lambda w: pl.BlockSpec((1, tm, w), lambda b, i: (b, i, 0))
    return pl.pallas_call(
        _post_kernel,
        grid=(bsz, n_tok // tm),
        in_specs=[row(D_MODEL), row(Q_WIDTH), row(Q_WIDTH), row(Q_WIDTH), row(3 * D_MODEL),
                  pl.BlockSpec((1, 6, D_MODEL), lambda b, i: (b, 0, 0)),
                  _const_spec((8, D_MODEL)),
                  _const_spec((Q_WIDTH, D_MODEL)), _const_spec((Q_WIDTH, D_MODEL)),
                  _const_spec((Q_WIDTH, D_MODEL)), _const_spec((D_MODEL, D_MODEL)),
                  _const_spec((D_MODEL, MLP_HIDDEN)), _const_spec((MLP_HIDDEN, D_MODEL))],
        out_specs=row(D_MODEL),
        out_shape=jax.ShapeDtypeStruct((bsz, n_tok, D_MODEL), F32),
        compiler_params=_params(("arbitrary", "arbitrary")),
        name="post_mlp",
    )(x, o_a, o_b, o_c, sg, mod, norms, wa, wb, wc, wo, w1, w2)


def _rope_tables(n_tok):
    pos = jnp.arange(n_tok)
    row = (pos // GRID_W).astype(F32)
    col = (pos % GRID_W).astype(F32)
    n_freq = HEAD_DIM // 4
    freqs = ROPE_THETA ** (-jnp.arange(n_freq, dtype=F32) / n_freq)
    ang = jnp.concatenate([row[:, None] * freqs, col[:, None] * freqs], axis=-1)
    cos = jnp.repeat(jnp.cos(ang), 2, axis=-1)
    sign = jnp.tile(jnp.array([-1.0, 1.0], F32), HEAD_DIM // 2)
    sin = jnp.repeat(jnp.sin(ang), 2, axis=-1) * sign
    return jnp.tile(cos, (1, LANES // HEAD_DIM)), jnp.tile(sin, (1, LANES // HEAD_DIM))


def _extended_w_in(w_in_l):
    sizes = (Q_WIDTH, KV_WIDTH, KV_WIDTH, Q_WIDTH, KV_WIDTH, KV_WIDTH, Q_WIDTH, Q_WIDTH, Q_WIDTH)
    parts, start = [], 0
    for n in sizes:
        parts.append(w_in_l[:, start:start + n])
        start += n
    qa, ka, va, qb, kb, vb, qc, kc, vc = parts
    gates = w_in_l[:, start:]
    swap = lambda w: w[:, jnp.arange(w.shape[1]) ^ 1]
    cols = [qa * Q_SCALE, ka, swap(qa) * Q_SCALE, swap(ka), va,
            qb, kb, swap(qb), swap(kb), vb,
            qc * Q_SCALE, kc, vc, gates]
    return jnp.concatenate(cols, axis=1).astype(BF16)


def kernel(x, c, ctx, c_ctx, w_ada, b_ada, norm_mix_pre, norm_mix_post, w_in, sink_a, qnorm_b,
           knorm_b, rpb_c, w_br_a, w_br_b, w_br_c, w_out, norm_mlp_pre, norm_mlp_post,
           w_mlp_in, w_mlp_out):
    bsz, seq, _ = x.shape
    n_ctx = ctx.shape[1]
    cvec = jnp.concatenate([c, c_ctx[None, :], jnp.zeros((8 - bsz - 1, D_MODEL), F32)], axis=0)
    mod_all = _ada_modulation(cvec, w_ada, b_ada)

    cos_lat, sin_lat = _rope_tables(seq)
    cos_ctx = jnp.ones((n_ctx, LANES), F32)
    sin_ctx = jnp.zeros((n_ctx, LANES), F32)
    head_id = jnp.arange(QK_W) // HEAD_DIM
    gmat = (head_id[:, None] == head_id[None, :]).astype(BF16)
    tile2 = lambda g: jnp.tile(g, LANES // HEAD_DIM)
    swap1 = lambda g: g[jnp.arange(HEAD_DIM) ^ 1]

    x_lat, x_ctx = x, ctx
    for l in range(DEPTH):
        last = l == DEPTH - 1
        mod_lat = mod_all[l, :bsz].reshape(bsz, 6, D_MODEL)
        mod_ctx = jnp.broadcast_to(mod_all[l, bsz].reshape(1, 6, D_MODEL), (bsz, 6, D_MODEL))
        gq, gk = qnorm_b[l], knorm_b[l]
        vecs = jnp.stack([tile2(gq) * Q_SCALE, tile2(swap1(gq)) * Q_SCALE,
                          tile2(gk), tile2(swap1(gk))]
                         + [jnp.zeros((LANES,), F32)] * 4)
        score_bound = (1.02 * HEAD_DIM * Q_SCALE * jnp.max(jnp.abs(gq)) * jnp.max(jnp.abs(gk))
                       ).reshape(1).astype(F32)
        gpre = norm_mix_pre[l][None, :]
        w_ext = _extended_w_in(w_in[l])
        norms = jnp.stack([norm_mix_post[l], norm_mlp_pre[l], norm_mlp_post[l]]
                          + [jnp.zeros((D_MODEL,), F32)] * 5)
        wa, wb, wc = (w.astype(BF16) for w in (w_br_a[l], w_br_b[l], w_br_c[l]))
        wo = w_out[l].astype(BF16)
        w1 = w_mlp_in[l].astype(BF16)
        w2 = w_mlp_out[l].astype(BF16)
        tabs = _neighbourhood_tables(rpb_c[l])
        sink = sink_a[l] * LOG2_E

        proj_lat = _input_projection(x_lat, mod_lat, gpre, vecs, cos_lat, sin_lat, gmat, w_ext)
        proj_ctx = _input_projection(x_ctx, mod_ctx, gpre, vecs, cos_ctx, sin_ctx, gmat, w_ext)
        qa, ka, va, qb, kb, vb, qc, kc, vc, sg = proj_lat
        qa_c, ka_c, va_c, qb_c, kb_c, vb_c, qc_c, kc_c, vc_c, sg_c = proj_ctx

        o_a = _attention_a(sink, qa, ka, va, ka_c, va_c)
        o_b = _attention_b(score_bound, qb, kb, vb, kb_c, vb_c)
        o_c = _attention_c(tabs, qc, kc, vc, kc_c, vc_c)
        x_lat = _post_block(x_lat, o_a, o_b, o_c, sg, mod_lat, norms, wa, wb, wc, wo, w1, w2)
        if not last:
            o_a_c, o_b_c, o_c_c = _attention_ctx(sink, qa_c, ka_c, va_c, qb_c, kb_c, vb_c,
                                                 qc_c, kc_c, vc_c)
            x_ctx = _post_block(x_ctx, o_a_c, o_b_c, o_c_c, sg_c, mod_ctx, norms,
                                wa, wb, wc, wo, w1, w2)
    return x_lat
```

```python
import jax
import jax.numpy as jnp
import numpy as np
from jax import lax
from jax.experimental import pallas as pl
from jax.experimental.pallas import tpu as pltpu

D_MODEL = 1024
SEQ = 8192
DEPTH = 2
GRID_W = 64
CTX_LEN = 256
HEAD_DIM = 64
A_Q_HEADS = 6
A_KV_HEADS = 2
WINDOW = 128
B_Q_HEADS = 6
B_KV_HEADS = 2
C_HEADS = 6
NA_KH_MAX = 8
NA_KW = 16
Q_WIDTH = 6 * HEAD_DIM
KV_WIDTH = 2 * HEAD_DIM
MLP_HIDDEN = 4 * D_MODEL
ROPE_THETA = 10000.0
NORM_EPS = 1e-6
NEG_INF = -1e30
SM_SCALE = HEAD_DIM ** -0.5
LOG2_E = 1.4426950408889634
Q_SCALE = SM_SCALE * LOG2_E

V7X_VMEM_LIMIT_BYTES = 56 * 1024 * 1024
LANES = 128
ROW_TILE = 256
WIN_BLOCK = 128
KV_CHUNK = 512
ADA_COLS = 1536
MAX_SAFE_SCORE_BOUND = 60.0

BF16 = jnp.bfloat16
F32 = jnp.float32

A_OFF = 0
B_OFF = A_OFF + 1152
C_OFF = B_OFF + 1152
G_OFF = C_OFF + 1152
EXT_COLS = G_OFF + 3 * D_MODEL
QK_W = Q_WIDTH + KV_WIDTH


def _params(semantics):
    return pltpu.CompilerParams(dimension_semantics=semantics,
                                vmem_limit_bytes=V7X_VMEM_LIMIT_BYTES)


def _const_spec(shape):
    nd = len(shape)
    return pl.BlockSpec(shape, lambda *_: (0,) * nd, pipeline_mode=pl.Buffered(1))


def _dot(a, b):
    return jnp.dot(a, b, preferred_element_type=F32)


def _rms(x):
    return x * lax.rsqrt(jnp.mean(x * x, axis=-1, keepdims=True) + NORM_EPS)


def _ada_kernel(c_ref, w_ref, b_ref, o_ref):
    c = c_ref[...]
    a = c / (1.0 + jnp.exp(-c))
    w = w_ref[0]
    a_hi = a.astype(BF16)
    a_lo = (a - a_hi.astype(F32)).astype(BF16)
    w_hi = w.astype(BF16)
    w_lo = (w - w_hi.astype(F32)).astype(BF16)
    acc = _dot(a_hi, w_hi) + _dot(a_hi, w_lo) + _dot(a_lo, w_hi)
    o_ref[0] = acc + b_ref[0]


def _ada_modulation(cvec, w_ada, b_ada):
    n_col = 6 * D_MODEL
    return pl.pallas_call(
        _ada_kernel,
        grid=(DEPTH, n_col // ADA_COLS),
        in_specs=[
            pl.BlockSpec((8, D_MODEL), lambda l, j: (0, 0)),
            pl.BlockSpec((1, D_MODEL, ADA_COLS), lambda l, j: (l, 0, j)),
            pl.BlockSpec((1, 1, ADA_COLS), lambda l, j: (l, 0, j)),
        ],
        out_specs=pl.BlockSpec((1, 8, ADA_COLS), lambda l, j: (l, 0, j)),
        out_shape=jax.ShapeDtypeStruct((DEPTH, 8, n_col), F32),
        compiler_params=_params(("arbitrary", "arbitrary")),
        name="ada_mod",
    )(cvec, w_ada, b_ada.reshape(DEPTH, 1, n_col))


def _store_heads(k_ref, k, n_heads):
    for h in range(n_heads):
        k_ref[0, h] = k[:, h * HEAD_DIM:(h + 1) * HEAD_DIM].astype(BF16)


def _store_values_t(vt_ref, v, n_heads):
    vt = v.T.astype(BF16)
    pad_row = lax.broadcasted_iota(jnp.int32, (HEAD_DIM, v.shape[0]), 0)
    ones_pad = jnp.where(pad_row == 0, 1.0, 0.0).astype(BF16)
    for h in range(n_heads):
        vt_ref[0, h, 0:HEAD_DIM, :] = vt[h * HEAD_DIM:(h + 1) * HEAD_DIM]
        vt_ref[0, h, HEAD_DIM:2 * HEAD_DIM, :] = ones_pad


def _proj_kernel(x_ref, mod_ref, gpre_ref, vecs_ref, cos_ref, sin_ref, gmat_ref, w_ref,
                 qa_ref, ka_ref, va_ref, qb_ref, kb_ref, vb_ref, qc_ref, kc_ref, vc_ref, sg_ref):
    x = x_ref[0]
    h = _rms(x) * gpre_ref[...]
    h = h * (1.0 + mod_ref[0, 1:2, :]) + mod_ref[0, 0:1, :]
    hb = h.astype(BF16)
    cos = cos_ref[...]
    sin = sin_ref[...]

    ra = _dot(hb, w_ref[:, A_OFF:A_OFF + 1152])
    for j in range(4):
        lo = j * LANES
        roped = ra[:, lo:lo + LANES] * cos + ra[:, QK_W + lo:QK_W + lo + LANES] * sin
        if j < 3:
            qa_ref[0, lo:lo + LANES, :] = roped.T.astype(BF16)
        else:
            _store_heads(ka_ref, roped, A_KV_HEADS)
    _store_values_t(va_ref, ra[:, 2 * QK_W:2 * QK_W + KV_WIDTH], A_KV_HEADS)

    rb = _dot(hb, w_ref[:, B_OFF:B_OFF + 1152])
    xb = rb[:, 0:QK_W]
    sq = xb * xb
    sq_hi = sq.astype(BF16)
    sq_lo = (sq - sq_hi.astype(F32)).astype(BF16)
    head_ms = (_dot(sq_hi, gmat_ref[...]) + _dot(sq_lo, gmat_ref[...])) * (1.0 / HEAD_DIM)
    rinv = lax.rsqrt(head_ms + NORM_EPS)
    cq, sq_t = cos * vecs_ref[0:1, :], sin * vecs_ref[1:2, :]
    ck, sk_t = cos * vecs_ref[2:3, :], sin * vecs_ref[3:4, :]
    for j in range(4):
        lo = j * LANES
        c_t, s_t = (cq, sq_t) if j < 3 else (ck, sk_t)
        roped = rinv[:, lo:lo + LANES] * (rb[:, lo:lo + LANES] * c_t
                                          + rb[:, QK_W + lo:QK_W + lo + LANES] * s_t)
        if j < 3:
            qb_ref[0, lo:lo + LANES, :] = roped.T.astype(BF16)
        else:
            _store_heads(kb_ref, roped, B_KV_HEADS)
    _store_values_t(vb_ref, rb[:, 2 * QK_W:2 * QK_W + KV_WIDTH], B_KV_HEADS)

    rc = _dot(hb, w_ref[:, C_OFF:C_OFF + 1152])
    qc_ref[0] = rc[:, 0:Q_WIDTH].T.astype(BF16)
    _store_heads(kc_ref, rc[:, Q_WIDTH:2 * Q_WIDTH], C_HEADS)
    _store_values_t(vc_ref, rc[:, 2 * Q_WIDTH:3 * Q_WIDTH], C_HEADS)

    for j in range(3):
        lo = G_OFF + j * D_MODEL
        g = _dot(hb, w_ref[:, lo:lo + D_MODEL])
        sg_ref[0, :, j * D_MODEL:(j + 1) * D_MODEL] = (1.0 / (1.0 + jnp.exp(-g))).astype(BF16)


def _input_projection(x, mod, gpre, vecs, cos_t, sin_t, gmat, w_ext):
    bsz, n_tok, _ = x.shape
    tm = ROW_TILE
    bf = lambda *s: jax.ShapeDtypeStruct(s, BF16)
    q_spec = pl.BlockSpec((1, Q_WIDTH, tm), lambda b, i: (b, 0, i))
    k_spec = lambda n: pl.BlockSpec((1, n, tm, HEAD_DIM), lambda b, i: (b, 0, i, 0))
    v_spec = lambda n: pl.BlockSpec((1, n, 2 * HEAD_DIM, tm), lambda b, i: (b, 0, 0, i))
    qkv_specs, qkv_shapes = [], []
    for n in (A_KV_HEADS, B_KV_HEADS, C_HEADS):
        qkv_specs += [q_spec, k_spec(n), v_spec(n)]
        qkv_shapes += [bf(bsz, Q_WIDTH, n_tok), bf(bsz, n, n_tok, HEAD_DIM),
                       bf(bsz, n, 2 * HEAD_DIM, n_tok)]
    return pl.pallas_call(
        _proj_kernel,
        grid=(bsz, n_tok // tm),
        in_specs=[
            pl.BlockSpec((1, tm, D_MODEL), lambda b, i: (b, i, 0)),
            pl.BlockSpec((1, 6, D_MODEL), lambda b, i: (b, 0, 0)),
            _const_spec((1, D_MODEL)),
            _const_spec((8, LANES)),
            pl.BlockSpec((tm, LANES), lambda b, i: (i, 0)),
            pl.BlockSpec((tm, LANES), lambda b, i: (i, 0)),
            _const_spec((QK_W, QK_W)),
            _const_spec((D_MODEL, EXT_COLS)),
        ],
        out_specs=qkv_specs + [pl.BlockSpec((1, tm, 3 * D_MODEL), lambda b, i: (b, i, 0))],
        out_shape=qkv_shapes + [bf(bsz, n_tok, 3 * D_MODEL)],
        compiler_params=_params(("arbitrary", "arbitrary")),
        name="in_proj",
    )(x, mod, gpre, vecs, cos_t, sin_t, gmat, w_ext)


def _flash_update_t(carry, k, vt, qt, bias=None):
    m, acc = carry
    st = _dot(k, qt)
    if bias is not None:
        st = st + bias
    m_new = jnp.maximum(m, jnp.max(st, axis=0, keepdims=True))
    alpha = jnp.exp2(m - m_new)
    p = jnp.exp2(st - m_new).astype(BF16)
    return m_new, alpha * acc + _dot(vt, p)


def _bounded_update_t(acc, k, vt, qt, bound):
    p = jnp.exp2(_dot(k, qt) - bound).astype(BF16)
    return acc + _dot(vt, p)


def _flash_init_t(n_q):
    return jnp.full((1, n_q), NEG_INF, F32), jnp.zeros((2 * HEAD_DIM, n_q), F32)


def _flash_finish_t(carry, sink=None):
    m, acc = carry
    denom = acc[HEAD_DIM:HEAD_DIM + 1]
    if sink is not None:
        denom = denom + jnp.exp2(sink - m)
    return acc[0:HEAD_DIM] / denom


def _softmax_pv_t(st, vt, sink=None):
    m = jnp.max(st, axis=0, keepdims=True)
    acc = _dot(vt, jnp.exp2(st - m).astype(BF16))
    denom = acc[HEAD_DIM:HEAD_DIM + 1]
    if sink is not None:
        denom = denom + jnp.exp2(sink - m)
    return acc[0:HEAD_DIM] / denom


def _stack_heads_t(qt_ref, g, group):
    return jnp.concatenate(
        [qt_ref[0, (g * group + j) * HEAD_DIM:(g * group + j + 1) * HEAD_DIM, :]
         for j in range(group)], axis=1)


def _sink_row(sink_ref, g, group, n_q):
    return jnp.concatenate(
        [jnp.full((1, n_q), sink_ref[g * group + j], F32) for j in range(group)], axis=1)


def _attn_a_kernel(sink_ref, qt_ref, k0_ref, k1_ref, k2_ref, k3_ref, v0_ref, v1_ref, v2_ref, v3_ref,
                   kc_ref, vc_ref, o_ref, st_ref):
    t = pl.program_id(1)
    nt = pl.num_programs(1)
    tq = o_ref.shape[1]
    n_win = 4 * WIN_BLOCK
    group = A_Q_HEADS // A_KV_HEADS
    kj = lax.broadcasted_iota(jnp.int32, (n_win, tq), 0)
    qi = lax.broadcasted_iota(jnp.int32, (n_win, tq), 1)
    rel = kj - WIN_BLOCK - qi
    valid = (rel <= WINDOW) & (rel >= -WINDOW)
    valid = valid & ((kj >= WIN_BLOCK) | (t > 0)) & ((kj < 3 * WIN_BLOCK) | (t < nt - 1))
    mask = jnp.where(valid, 0.0, NEG_INF).astype(F32)
    mask = jnp.concatenate([mask] * group, axis=1)

    def scores_to(g):
        keys = jnp.concatenate([r[0, g] for r in (k0_ref, k1_ref, k2_ref, k3_ref, kc_ref)], axis=0)
        st = _dot(keys, _stack_heads_t(qt_ref, g, group))
        st_ref[g, 0:n_win] = st[0:n_win] + mask
        st_ref[g, n_win:] = st[n_win:]

    def attend(g):
        vals_t = jnp.concatenate([r[0, g] for r in (v0_ref, v1_ref, v2_ref, v3_ref, vc_ref)], axis=1)
        og = _softmax_pv_t(st_ref[g], vals_t, sink=_sink_row(sink_ref, g, group, tq))
        return [og[:, j * tq:(j + 1) * tq] for j in range(group)]

    heads_t = []
    scores_to(0)
    for g in range(A_KV_HEADS):
        if g + 1 < A_KV_HEADS:
            scores_to(g + 1)
        heads_t += attend(g)
    o_ref[0] = jnp.concatenate(heads_t, axis=0).T.astype(BF16)


def _attention_a(sink, qt, k, vt, k_ctx, vt_ctx):
    bsz, _, n_tok = qt.shape
    tq = ROW_TILE
    nb = n_tok // WIN_BLOCK
    per = tq // WIN_BLOCK
    slot = lambda s: (lambda b, i: jnp.clip(per * i - 1 + s, 0, nb - 1))
    k_spec = lambda s: pl.BlockSpec((1, A_KV_HEADS, WIN_BLOCK, HEAD_DIM),
                                    lambda b, i: (b, 0, slot(s)(b, i), 0))
    v_spec = lambda s: pl.BlockSpec((1, A_KV_HEADS, 2 * HEAD_DIM, WIN_BLOCK),
                                    lambda b, i: (b, 0, 0, slot(s)(b, i)))
    return pl.pallas_call(
        _attn_a_kernel,
        grid=(bsz, n_tok // tq),
        in_specs=[pl.BlockSpec(memory_space=pltpu.SMEM),
                  pl.BlockSpec((1, Q_WIDTH, tq), lambda b, i: (b, 0, i))]
                 + [k_spec(s) for s in range(4)] + [v_spec(s) for s in range(4)]
                 + [pl.BlockSpec((1, A_KV_HEADS, CTX_LEN, HEAD_DIM), lambda b, i: (b, 0, 0, 0)),
                    pl.BlockSpec((1, A_KV_HEADS, 2 * HEAD_DIM, CTX_LEN), lambda b, i: (b, 0, 0, 0))],
        out_specs=pl.BlockSpec((1, tq, Q_WIDTH), lambda b, i: (b, i, 0)),
        out_shape=jax.ShapeDtypeStruct((bsz, n_tok, Q_WIDTH), BF16),
        scratch_shapes=[pltpu.VMEM((A_KV_HEADS, 4 * WIN_BLOCK + CTX_LEN,
                                    (A_Q_HEADS // A_KV_HEADS) * tq), F32)],
        compiler_params=_params(("arbitrary", "arbitrary")),
        name="attn_window",
    )(sink, qt, k, k, k, k, vt, vt, vt, vt, k_ctx, vt_ctx)


def _attn_b_kernel(bound_ref, qt_ref, kc_ref, vtc_ref, k_ref, vt_ref, o_ref, st_ref):
    tq = o_ref.shape[1]
    n_chunks = k_ref.shape[2] // KV_CHUNK
    group = B_Q_HEADS // B_KV_HEADS
    groups = range(B_KV_HEADS)
    qts = [_stack_heads_t(qt_ref, g, group) for g in groups]

    def chunk(c, g):
        start = pl.multiple_of(c * KV_CHUNK, KV_CHUNK)
        return k_ref[0, g, pl.ds(start, KV_CHUNK), :], vt_ref[0, g, :, pl.ds(start, KV_CHUNK)]

    def write(outs_t):
        heads_t = [og[:, j * tq:(j + 1) * tq] for og in outs_t for j in range(group)]
        o_ref[0] = jnp.concatenate(heads_t, axis=0).T.astype(BF16)

    bound = bound_ref[0]
    bounded = bound <= MAX_SAFE_SCORE_BOUND

    @pl.when(bounded)
    def _():
        def scores_to(slot, c):
            for g in groups:
                st_ref[slot, g] = _dot(chunk(c, g)[0], qts[g])

        def accumulate(accs, slot, c):
            return tuple(accs[g] + _dot(chunk(c, g)[1],
                                        jnp.exp2(st_ref[slot, g] - bound).astype(BF16))
                         for g in groups)

        def body(i, accs):
            c = 2 * i
            scores_to(1, c + 1)
            accs = accumulate(accs, 0, c)
            scores_to(0, c + 2)
            return accumulate(accs, 1, c + 1)

        scores_to(0, 0)
        accs = tuple(_bounded_update_t(jnp.zeros((2 * HEAD_DIM, group * tq), F32),
                                       kc_ref[0, g], vtc_ref[0, g], qts[g], bound) for g in groups)
        accs = lax.fori_loop(0, n_chunks // 2 - 1, body, accs)
        scores_to(1, n_chunks - 1)
        accs = accumulate(accs, 0, n_chunks - 2)
        accs = accumulate(accs, 1, n_chunks - 1)
        write([acc[0:HEAD_DIM] / acc[HEAD_DIM:HEAD_DIM + 1] for acc in accs])

    @pl.when(jnp.logical_not(bounded))
    def _():
        carries = tuple(_flash_update_t(_flash_init_t(group * tq), kc_ref[0, g], vtc_ref[0, g], qts[g])
                        for g in groups)
        carries = lax.fori_loop(
            0, n_chunks,
            lambda c, carries: tuple(_flash_update_t(carries[g], *chunk(c, g), qts[g])
                                     for g in groups), carries)
        write([_flash_finish_t(carry) for carry in carries])


def _attention_b(bound, qt, k, vt, k_ctx, vt_ctx):
    bsz, _, n_tok = qt.shape
    tq = ROW_TILE
    keys = lambda n: pl.BlockSpec((1, B_KV_HEADS, n, HEAD_DIM), lambda b, i: (b, 0, 0, 0))
    vals = lambda n: pl.BlockSpec((1, B_KV_HEADS, 2 * HEAD_DIM, n), lambda b, i: (b, 0, 0, 0))
    return pl.pallas_call(
        _attn_b_kernel,
        grid=(bsz, n_tok // tq),
        in_specs=[pl.BlockSpec(memory_space=pltpu.SMEM),
                  pl.BlockSpec((1, Q_WIDTH, tq), lambda b, i: (b, 0, i)),
                  keys(CTX_LEN), vals(CTX_LEN), keys(n_tok), vals(n_tok)],
        out_specs=pl.BlockSpec((1, tq, Q_WIDTH), lambda b, i: (b, i, 0)),
        out_shape=jax.ShapeDtypeStruct((bsz, n_tok, Q_WIDTH), BF16),
        scratch_shapes=[pltpu.VMEM((2, B_KV_HEADS, KV_CHUNK, (B_Q_HEADS // B_KV_HEADS) * tq), F32)],
        compiler_params=_params(("arbitrary", "arbitrary")),
        name="attn_global",
    )(bound, qt, k_ctx, vt_ctx, k, vt)


def _attn_c_kernel(tab_ref, qt_ref, k0_ref, k1_ref, k2_ref, v0_ref, v1_ref, v2_ref,
                   kc_ref, vc_ref, o_ref, st_ref):
    n_win = tab_ref.shape[2]

    def scores_to(h):
        keys = jnp.concatenate([r[0, h] for r in (k0_ref, k1_ref, k2_ref, kc_ref)], axis=0)
        st = _dot(keys, qt_ref[0, h * HEAD_DIM:(h + 1) * HEAD_DIM, :])
        st_ref[h % 2, 0:n_win] = st[0:n_win] + tab_ref[0, h]
        st_ref[h % 2, n_win:] = st[n_win:]

    def attend(h):
        vals_t = jnp.concatenate([r[0, h] for r in (v0_ref, v1_ref, v2_ref, vc_ref)], axis=1)
        return _softmax_pv_t(st_ref[h % 2], vals_t)

    heads_t = []
    scores_to(0)
    for h in range(C_HEADS):
        if h + 1 < C_HEADS:
            scores_to(h + 1)
        heads_t.append(attend(h))
    o_ref[0] = jnp.concatenate(heads_t, axis=0).T.astype(BF16)


def _attention_c(tab, qt, k, vt, k_ctx, vt_ctx):
    bsz, _, n_tok = qt.shape
    tq = ROW_TILE
    nt = n_tok // tq
    slot = lambda s: (lambda b, i: jnp.clip(i - 1 + s, 0, nt - 1))
    k_spec = lambda s: pl.BlockSpec((1, C_HEADS, tq, HEAD_DIM), lambda b, i: (b, 0, slot(s)(b, i), 0))
    v_spec = lambda s: pl.BlockSpec((1, C_HEADS, 2 * HEAD_DIM, tq),
                                    lambda b, i: (b, 0, 0, slot(s)(b, i)))
    variant = lambda b, i: (jnp.where(i == 0, 0, jnp.where(i == nt - 1, 2, 1)), 0, 0, 0)
    return pl.pallas_call(
        _attn_c_kernel,
        grid=(bsz, nt),
        in_specs=[pl.BlockSpec((1, C_HEADS, 3 * tq, tq), variant),
                  pl.BlockSpec((1, Q_WIDTH, tq), lambda b, i: (b, 0, i))]
                 + [k_spec(s) for s in range(3)] + [v_spec(s) for s in range(3)]
                 + [pl.BlockSpec((1, C_HEADS, CTX_LEN, HEAD_DIM), lambda b, i: (b, 0, 0, 0)),
                    pl.BlockSpec((1, C_HEADS, 2 * HEAD_DIM, CTX_LEN), lambda b, i: (b, 0, 0, 0))],
        out_specs=pl.BlockSpec((1, tq, Q_WIDTH), lambda b, i: (b, i, 0)),
        out_shape=jax.ShapeDtypeStruct((bsz, n_tok, Q_WIDTH), BF16),
        scratch_shapes=[pltpu.VMEM((2, 3 * tq + CTX_LEN, tq), F32)],
        compiler_params=_params(("arbitrary", "arbitrary")),
        name="attn_nbr",
    )(tab, qt, k, k, k, vt, vt, vt, k_ctx, vt_ctx)


def _na_table_kernel(rowmat_ref, o_ref):
    rows = SEQ // GRID_W
    rpt = ROW_TILE // GRID_W
    nt = rows // rpt
    kh = min(NA_KH_MAX, rows)
    masked = jnp.full((GRID_W, GRID_W), NEG_INF, F32)
    for v, t in enumerate((0, 1, nt - 1)):
        for rl in range(rpt):
            r = rpt * t + rl
            rs = min(max(r - kh // 2, 0), rows - kh)
            for sk in range(3 * rpt):
                kr = rpt * (t - 1) + sk
                blk = rowmat_ref[0, kr - r + NA_KH_MAX - 1] if rs <= kr < rs + kh else masked
                o_ref[v, 0, sk * GRID_W:(sk + 1) * GRID_W, rl * GRID_W:(rl + 1) * GRID_W] = blk


def _neighbourhood_tables(rpb):
    c = np.arange(GRID_W)
    ws = np.clip(c - NA_KW // 2, 0, GRID_W - NA_KW)
    valid = (c[None, :] >= ws[:, None]) & (c[None, :] < ws[:, None] + NA_KW)
    coff = np.clip(c[None, :] - c[:, None] + NA_KW - 1, 0, 2 * NA_KW - 2)
    onehot = (coff[:, :, None] == np.arange(2 * NA_KW - 1)).astype(np.float32)
    rowmat = jnp.einsum("hij,ckj->hikc", rpb * LOG2_E, onehot, precision=lax.Precision.HIGHEST)
    rowmat = jnp.where(valid.T, rowmat, NEG_INF).astype(F32)
    n_off = 2 * NA_KH_MAX - 1
    return pl.pallas_call(
        _na_table_kernel,
        grid=(C_HEADS,),
        in_specs=[pl.BlockSpec((1, n_off, GRID_W, GRID_W), lambda h: (h, 0, 0, 0))],
        out_specs=pl.BlockSpec((3, 1, 3 * ROW_TILE, ROW_TILE), lambda h: (0, h, 0, 0)),
        out_shape=jax.ShapeDtypeStruct((3, C_HEADS, 3 * ROW_TILE, ROW_TILE), F32),
        compiler_params=_params(("arbitrary",)),
        name="na_tables",
    )(rowmat)


def _attn_ctx_kernel(sink_ref, qa_ref, ka_ref, va_ref, qb_ref, kb_ref, vb_ref,
                     qc_ref, kc_ref, vc_ref, oa_ref, ob_ref, oc_ref):
    group = A_Q_HEADS // A_KV_HEADS
    n_q = oa_ref.shape[1]
    outs = ([], [], [])
    for h in range(A_Q_HEADS):
        g = h // group
        hs = slice(h * HEAD_DIM, (h + 1) * HEAD_DIM)
        init = _flash_init_t(n_q)
        outs[0].append(_flash_finish_t(
            _flash_update_t(init, ka_ref[0, g], va_ref[0, g], qa_ref[0, hs, :]),
            sink=jnp.full((1, n_q), sink_ref[h], F32)))
        outs[1].append(_flash_finish_t(
            _flash_update_t(init, kb_ref[0, g], vb_ref[0, g], qb_ref[0, hs, :])))
        outs[2].append(_flash_finish_t(
            _flash_update_t(init, kc_ref[0, h], vc_ref[0, h], qc_ref[0, hs, :])))
    for o_ref, heads_t in zip((oa_ref, ob_ref, oc_ref), outs):
        o_ref[0] = jnp.concatenate(heads_t, axis=0).T.astype(BF16)


def _attention_ctx(sink, qa, ka, va, qb, kb, vb, qc, kc, vc):
    bsz = qa.shape[0]
    q_spec = pl.BlockSpec((1, Q_WIDTH, CTX_LEN), lambda b: (b, 0, 0))
    k_spec = lambda n: pl.BlockSpec((1, n, CTX_LEN, HEAD_DIM), lambda b: (b, 0, 0, 0))
    v_spec = lambda n: pl.BlockSpec((1, n, 2 * HEAD_DIM, CTX_LEN), lambda b: (b, 0, 0, 0))
    out = jax.ShapeDtypeStruct((bsz, CTX_LEN, Q_WIDTH), BF16)
    specs = [pl.BlockSpec(memory_space=pltpu.SMEM)]
    for n in (A_KV_HEADS, B_KV_HEADS, C_HEADS):
        specs += [q_spec, k_spec(n), v_spec(n)]
    return pl.pallas_call(
        _attn_ctx_kernel,
        grid=(bsz,),
        in_specs=specs,
        out_specs=[pl.BlockSpec((1, CTX_LEN, Q_WIDTH), lambda b: (b, 0, 0))] * 3,
        out_shape=[out] * 3,
        compiler_params=_params(("arbitrary",)),
        name="attn_ctx",
    )(sink, qa, ka, va, qb, kb, vb, qc, kc, vc)


def _post_kernel(x_ref, oa_ref, ob_ref, oc_ref, sg_ref, mod_ref, norms_ref,
                 wa_ref, wb_ref, wc_ref, wo_ref, w1_ref, w2_ref, o_ref):
    x = x_ref[0]
    merged = None
    for j, (o_r, w_r) in enumerate(((oa_ref, wa_ref), (ob_ref, wb_ref), (oc_ref, wc_ref))):
        gate = sg_ref[0, :, j * D_MODEL:(j + 1) * D_MODEL].astype(F32)
        term = gate * _dot(o_r[0], w_r[...])
        merged = term if merged is None else merged + term
    y = _dot(merged.astype(BF16), wo_ref[...])
    x1 = x + mod_ref[0, 2:3, :] * (_rms(y) * norms_ref[0:1, :])
    h2 = _rms(x1) * norms_ref[1:2, :]
    h2 = (h2 * (1.0 + mod_ref[0, 4:5, :]) + mod_ref[0, 3:4, :]).astype(BF16)
    z = None
    n_chunk = MLP_HIDDEN // D_MODEL
    for j in range(n_chunk):
        cs = slice(j * D_MODEL, (j + 1) * D_MODEL)
        a = jnp.maximum(_dot(h2, w1_ref[:, cs]), 0.0)
        zj = _dot((a * a).astype(BF16), w2_ref[cs, :])
        z = zj if z is None else z + zj
    o_ref[0] = x1 + mod_ref[0, 5:6, :] * (_rms(z) * norms_ref[2:3, :])


def _post_block(x, o_a, o_b, o_c, sg, mod, norms, wa, wb, wc, wo, w1, w2):
    bsz, n_tok, _ = x.shape
    tm = ROW_TILE
    row = lambda w: pl.BlockSpec((1, tm, w), lambda b, i: (b, i, 0))
    return pl.pallas_call(
        _post_kernel,
        grid=(bsz, n_tok // tm),
        in_specs=[row(D_MODEL), row(Q_WIDTH), row(Q_WIDTH), row(Q_WIDTH), row(3 * D_MODEL),
                  pl.BlockSpec((1, 6, D_MODEL), lambda b, i: (b, 0, 0)),
                  _const_spec((8, D_MODEL)),
                  _const_spec((Q_WIDTH, D_MODEL)), _const_spec((Q_WIDTH, D_MODEL)),
                  _const_spec((Q_WIDTH, D_MODEL)), _const_spec((D_MODEL, D_MODEL)),
                  _const_spec((D_MODEL, MLP_HIDDEN)), _const_spec((MLP_HIDDEN, D_MODEL))],
        out_specs=row(D_MODEL),
        out_shape=jax.ShapeDtypeStruct((bsz, n_tok, D_MODEL), F32),
        compiler_params=_params(("arbitrary", "arbitrary")),
        name="post_mlp",
    )(x, o_a, o_b, o_c, sg, mod, norms, wa, wb, wc, wo, w1, w2)


def _rope_tables(n_tok):
    pos = jnp.arange(n_tok)
    row = (pos // GRID_W).astype(F32)
    col = (pos % GRID_W).astype(F32)
    n_freq = HEAD_DIM // 4
    freqs = ROPE_THETA ** (-jnp.arange(n_freq, dtype=F32) / n_freq)
    ang = jnp.concatenate([row[:, None] * freqs, col[:, None] * freqs], axis=-1)
    cos = jnp.repeat(jnp.cos(ang), 2, axis=-1)
    sign = jnp.tile(jnp.array([-1.0, 1.0], F32), HEAD_DIM // 2)
    sin = jnp.repeat(jnp.sin(ang), 2, axis=-1) * sign
    return jnp.tile(cos, (1, LANES // HEAD_DIM)), jnp.tile(sin, (1, LANES // HEAD_DIM))


def _extended_w_in(w_in_l):
    sizes = (Q_WIDTH, KV_WIDTH, KV_WIDTH, Q_WIDTH, KV_WIDTH, KV_WIDTH, Q_WIDTH, Q_WIDTH, Q_WIDTH)
    parts, start = [], 0
    for n in sizes:
        parts.append(w_in_l[:, start:start + n])
        start += n
    qa, ka, va, qb, kb, vb, qc, kc, vc = parts
    gates = w_in_l[:, start:]
    swap = lambda w: w[:, jnp.arange(w.shape[1]) ^ 1]
    cols = [qa * Q_SCALE, ka, swap(qa) * Q_SCALE, swap(ka), va,
            qb, kb, swap(qb), swap(kb), vb,
            qc * Q_SCALE, kc, vc, gates]
    return jnp.concatenate(cols, axis=1).astype(BF16)


def kernel(x, c, ctx, c_ctx, w_ada, b_ada, norm_mix_pre, norm_mix_post, w_in, sink_a, qnorm_b,
           knorm_b, rpb_c, w_br_a, w_br_b, w_br_c, w_out, norm_mlp_pre, norm_mlp_post,
           w_mlp_in, w_mlp_out):
    bsz, seq, _ = x.shape
    n_ctx = ctx.shape[1]
    cvec = jnp.concatenate([c, c_ctx[None, :], jnp.zeros((8 - bsz - 1, D_MODEL), F32)], axis=0)
    mod_all = _ada_modulation(cvec, w_ada, b_ada)

    cos_lat, sin_lat = _rope_tables(seq)
    cos_ctx = jnp.ones((n_ctx, LANES), F32)
    sin_ctx = jnp.zeros((n_ctx, LANES), F32)
    head_id = jnp.arange(QK_W) // HEAD_DIM
    gmat = (head_id[:, None] == head_id[None, :]).astype(BF16)
    tile2 = lambda g: jnp.tile(g, LANES // HEAD_DIM)
    swap1 = lambda g: g[jnp.arange(HEAD_DIM) ^ 1]

    x_lat, x_ctx = x, ctx
    for l in range(DEPTH):
        last = l == DEPTH - 1
        mod_lat = mod_all[l, :bsz].reshape(bsz, 6, D_MODEL)
        mod_ctx = jnp.broadcast_to(mod_all[l, bsz].reshape(1, 6, D_MODEL), (bsz, 6, D_MODEL))
        gq, gk = qnorm_b[l], knorm_b[l]
        vecs = jnp.stack([tile2(gq) * Q_SCALE, tile2(swap1(gq)) * Q_SCALE,
                          tile2(gk), tile2(swap1(gk))]
                         + [jnp.zeros((LANES,), F32)] * 4)
        score_bound = (1.02 * HEAD_DIM * Q_SCALE * jnp.max(jnp.abs(gq)) * jnp.max(jnp.abs(gk))
                       ).reshape(1).astype(F32)
        gpre = norm_mix_pre[l][None, :]
        w_ext = _extended_w_in(w_in[l])
        norms = jnp.stack([norm_mix_post[l], norm_mlp_pre[l], norm_mlp_post[l]]
                          + [jnp.zeros((D_MODEL,), F32)] * 5)
        wa, wb, wc = (w.astype(BF16) for w in (w_br_a[l], w_br_b[l], w_br_c[l]))
        wo = w_out[l].astype(BF16)
        w1 = w_mlp_in[l].astype(BF16)
        w2 = w_mlp_out[l].astype(BF16)
        tabs = _neighbourhood_tables(rpb_c[l])
        sink = sink_a[l] * LOG2_E

        proj_lat = _input_projection(x_lat, mod_lat, gpre, vecs, cos_lat, sin_lat, gmat, w_ext)
        proj_ctx = _input_projection(x_ctx, mod_ctx, gpre, vecs, cos_ctx, sin_ctx, gmat, w_ext)
        qa, ka, va, qb, kb, vb, qc, kc, vc, sg = proj_lat
        qa_c, ka_c, va_c, qb_c, kb_c, vb_c, qc_c, kc_c, vc_c, sg_c = proj_ctx

        o_a = _attention_a(sink, qa, ka, va, ka_c, va_c)
        o_b = _attention_b(score_bound, qb, kb, vb, kb_c, vb_c)
        o_c = _attention_c(tabs, qc, kc, vc, kc_c, vc_c)
        x_lat = _post_block(x_lat, o_a, o_b, o_c, sg, mod_lat, norms, wa, wb, wc, wo, w1, w2)
        if not last:
            o_a_c, o_b_c, o_c_c = _attention_ctx(sink, qa_c, ka_c, va_c, qb_c, kb_c, vb_c,
                                                 qc_c, kc_c, vc_c)
            x_ctx = _post_block(x_ctx, o_a_c, o_b_c, o_c_c, sg_c, mod_ctx, norms,
                                wa, wb, wc, wo, w1, w2)
    return x_lat
```

```python
import jax
import jax.numpy as jnp
import numpy as np
from jax import lax
from jax.experimental import pallas as pl
from jax.experimental.pallas import tpu as pltpu

D_MODEL = 1024
SEQ = 8192
DEPTH = 2
GRID_W = 64
CTX_LEN = 256
HEAD_DIM = 64
A_Q_HEADS = 6
A_KV_HEADS = 2
WINDOW = 128
B_Q_HEADS = 6
B_KV_HEADS = 2
C_HEADS = 6
NA_KH_MAX = 8
NA_KW = 16
Q_WIDTH = 6 * HEAD_DIM
KV_WIDTH = 2 * HEAD_DIM
MLP_HIDDEN = 4 * D_MODEL
ROPE_THETA = 10000.0
NORM_EPS = 1e-6
NEG_INF = -1e30
SM_SCALE = HEAD_DIM ** -0.5
LOG2_E = 1.4426950408889634
Q_SCALE = SM_SCALE * LOG2_E

V7X_VMEM_LIMIT_BYTES = 56 * 1024 * 1024
LANES = 128
ROW_TILE = 256
WIN_BLOCK = 128
KV_CHUNK = 512
GLOBAL_Q_TILE = 512
ADA_COLS = 1536
MAX_SAFE_SCORE_BOUND = 60.0

BF16 = jnp.bfloat16
F32 = jnp.float32

A_OFF = 0
B_OFF = A_OFF + 1152
C_OFF = B_OFF + 1152
G_OFF = C_OFF + 1152
EXT_COLS = G_OFF + 3 * D_MODEL
QK_W = Q_WIDTH + KV_WIDTH


def _params(semantics):
    return pltpu.CompilerParams(dimension_semantics=semantics,
                                vmem_limit_bytes=V7X_VMEM_LIMIT_BYTES)


def _const_spec(shape):
    nd = len(shape)
    return pl.BlockSpec(shape, lambda *_: (0,) * nd, pipeline_mode=pl.Buffered(1))


def _dot(a, b):
    return jnp.dot(a, b, preferred_element_type=F32)


def _rms(x):
    return x * lax.rsqrt(jnp.mean(x * x, axis=-1, keepdims=True) + NORM_EPS)


def _ada_kernel(c_ref, w_ref, b_ref, o_ref):
    c = c_ref[...]
    a = c / (1.0 + jnp.exp(-c))
    w = w_ref[0]
    a_hi = a.astype(BF16)
    a_lo = (a - a_hi.astype(F32)).astype(BF16)
    w_hi = w.astype(BF16)
    w_lo = (w - w_hi.astype(F32)).astype(BF16)
    acc = _dot(a_hi, w_hi) + _dot(a_hi, w_lo) + _dot(a_lo, w_hi)
    o_ref[0] = acc + b_ref[0]


def _ada_modulation(cvec, w_ada, b_ada):
    n_col = 6 * D_MODEL
    return pl.pallas_call(
        _ada_kernel,
        grid=(DEPTH, n_col // ADA_COLS),
        in_specs=[
            pl.BlockSpec((8, D_MODEL), lambda l, j: (0, 0)),
            pl.BlockSpec((1, D_MODEL, ADA_COLS), lambda l, j: (l, 0, j)),
            pl.BlockSpec((1, 1, ADA_COLS), lambda l, j: (l, 0, j)),
        ],
        out_specs=pl.BlockSpec((1, 8, ADA_COLS), lambda l, j: (l, 0, j)),
        out_shape=jax.ShapeDtypeStruct((DEPTH, 8, n_col), F32),
        compiler_params=_params(("arbitrary", "arbitrary")),
        name="ada_mod",
    )(cvec, w_ada, b_ada.reshape(DEPTH, 1, n_col))


def _store_heads(k_ref, k, n_heads):
    for h in range(n_heads):
        k_ref[0, h] = k[:, h * HEAD_DIM:(h + 1) * HEAD_DIM].astype(BF16)


def _store_values_t(vt_ref, v, n_heads):
    vt = v.T.astype(BF16)
    pad_row = lax.broadcasted_iota(jnp.int32, (HEAD_DIM, v.shape[0]), 0)
    ones_pad = jnp.where(pad_row == 0, 1.0, 0.0).astype(BF16)
    for h in range(n_heads):
        vt_ref[0, h, 0:HEAD_DIM, :] = vt[h * HEAD_DIM:(h + 1) * HEAD_DIM]
        vt_ref[0, h, HEAD_DIM:2 * HEAD_DIM, :] = ones_pad


def _proj_kernel(x_ref, mod_ref, gpre_ref, vecs_ref, cos_ref, sin_ref, gmat_ref, w_ref,
                 qa_ref, ka_ref, va_ref, qb_ref, kb_ref, vb_ref, qc_ref, kc_ref, vc_ref, sg_ref):
    x = x_ref[0]
    h = _rms(x) * gpre_ref[...]
    h = h * (1.0 + mod_ref[0, 1:2, :]) + mod_ref[0, 0:1, :]
    hb = h.astype(BF16)
    cos = cos_ref[...]
    sin = sin_ref[...]

    ra = _dot(hb, w_ref[:, A_OFF:A_OFF + 1152])
    for j in range(4):
        lo = j * LANES
        roped = ra[:, lo:lo + LANES] * cos + ra[:, QK_W + lo:QK_W + lo + LANES] * sin
        if j < 3:
            qa_ref[0, lo:lo + LANES, :] = roped.T.astype(BF16)
        else:
            _store_heads(ka_ref, roped, A_KV_HEADS)
    _store_values_t(va_ref, ra[:, 2 * QK_W:2 * QK_W + KV_WIDTH], A_KV_HEADS)

    rb = _dot(hb, w_ref[:, B_OFF:B_OFF + 1152])
    xb = rb[:, 0:QK_W]
    sq = xb * xb
    sq_hi = sq.astype(BF16)
    sq_lo = (sq - sq_hi.astype(F32)).astype(BF16)
    head_ms = (_dot(sq_hi, gmat_ref[...]) + _dot(sq_lo, gmat_ref[...])) * (1.0 / HEAD_DIM)
    rinv = lax.rsqrt(head_ms + NORM_EPS)
    cq, sq_t = cos * vecs_ref[0:1, :], sin * vecs_ref[1:2, :]
    ck, sk_t = cos * vecs_ref[2:3, :], sin * vecs_ref[3:4, :]
    for j in range(4):
        lo = j * LANES
        c_t, s_t = (cq, sq_t) if j < 3 else (ck, sk_t)
        roped = rinv[:, lo:lo + LANES] * (rb[:, lo:lo + LANES] * c_t
                                          + rb[:, QK_W + lo:QK_W + lo + LANES] * s_t)
        if j < 3:
            qb_ref[0, lo:lo + LANES, :] = roped.T.astype(BF16)
        else:
            _store_heads(kb_ref, roped, B_KV_HEADS)
    _store_values_t(vb_ref, rb[:, 2 * QK_W:2 * QK_W + KV_WIDTH], B_KV_HEADS)

    rc = _dot(hb, w_ref[:, C_OFF:C_OFF + 1152])
    qc_ref[0] = rc[:, 0:Q_WIDTH].T.astype(BF16)
    _store_heads(kc_ref, rc[:, Q_WIDTH:2 * Q_WIDTH], C_HEADS)
    _store_values_t(vc_ref, rc[:, 2 * Q_WIDTH:3 * Q_WIDTH], C_HEADS)

    for j in range(3):
        lo = G_OFF + j * D_MODEL
        g = _dot(hb, w_ref[:, lo:lo + D_MODEL])
        sg_ref[0, :, j * D_MODEL:(j + 1) * D_MODEL] = (1.0 / (1.0 + jnp.exp(-g))).astype(BF16)


def _input_projection(x, mod, gpre, vecs, cos_t, sin_t, gmat, w_ext):
    bsz, n_tok, _ = x.shape
    tm = ROW_TILE
    bf = lambda *s: jax.ShapeDtypeStruct(s, BF16)
    q_spec = pl.BlockSpec((1, Q_WIDTH, tm), lambda b, i: (b, 0, i))
    k_spec = lambda n: pl.BlockSpec((1, n, tm, HEAD_DIM), lambda b, i: (b, 0, i, 0))
    v_spec = lambda n: pl.BlockSpec((1, n, 2 * HEAD_DIM, tm), lambda b, i: (b, 0, 0, i))
    qkv_specs, qkv_shapes = [], []
    for n in (A_KV_HEADS, B_KV_HEADS, C_HEADS):
        qkv_specs += [q_spec, k_spec(n), v_spec(n)]
        qkv_shapes += [bf(bsz, Q_WIDTH, n_tok), bf(bsz, n, n_tok, HEAD_DIM),
                       bf(bsz, n, 2 * HEAD_DIM, n_tok)]
    return pl.pallas_call(
        _proj_kernel,
        grid=(bsz, n_tok // tm),
        in_specs=[
            pl.BlockSpec((1, tm, D_MODEL), lambda b, i: (b, i, 0)),
            pl.BlockSpec((1, 6, D_MODEL), lambda b, i: (b, 0, 0)),
            _const_spec((1, D_MODEL)),
            _const_spec((8, LANES)),
            pl.BlockSpec((tm, LANES), lambda b, i: (i, 0)),
            pl.BlockSpec((tm, LANES), lambda b, i: (i, 0)),
            _const_spec((QK_W, QK_W)),
            _const_spec((D_MODEL, EXT_COLS)),
        ],
        out_specs=qkv_specs + [pl.BlockSpec((1, tm, 3 * D_MODEL), lambda b, i: (b, i, 0))],
        out_shape=qkv_shapes + [bf(bsz, n_tok, 3 * D_MODEL)],
        compiler_params=_params(("arbitrary", "arbitrary")),
        name="in_proj",
    )(x, mod, gpre, vecs, cos_t, sin_t, gmat, w_ext)


def _flash_update_t(carry, k, vt, qt, bias=None):
    m, acc = carry
    st = _dot(k, qt)
    if bias is not None:
        st = st + bias
    m_new = jnp.maximum(m, jnp.max(st, axis=0, keepdims=True))
    alpha = jnp.exp2(m - m_new)
    p = jnp.exp2(st - m_new).astype(BF16)
    return m_new, alpha * acc + _dot(vt, p)


def _bounded_update_t(acc, k, vt, qt, bound):
    p = jnp.exp2(_dot(k, qt) - bound).astype(BF16)
    return acc + _dot(vt, p)


def _flash_init_t(n_q):
    return jnp.full((1, n_q), NEG_INF, F32), jnp.zeros((2 * HEAD_DIM, n_q), F32)


def _flash_finish_t(carry, sink=None):
    m, acc = carry
    denom = acc[HEAD_DIM:HEAD_DIM + 1]
    if sink is not None:
        denom = denom + jnp.exp2(sink - m)
    return acc[0:HEAD_DIM] / denom


def _softmax_pv_t(st, vt, sink=None):
    m = jnp.max(st, axis=0, keepdims=True)
    acc = _dot(vt, jnp.exp2(st - m).astype(BF16))
    denom = acc[HEAD_DIM:HEAD_DIM + 1]
    if sink is not None:
        denom = denom + jnp.exp2(sink - m)
    return acc[0:HEAD_DIM] / denom


def _stack_heads_t(qt_ref, g, group):
    return jnp.concatenate(
        [qt_ref[0, (g * group + j) * HEAD_DIM:(g * group + j + 1) * HEAD_DIM, :]
         for j in range(group)], axis=1)


def _sink_row(sink_ref, g, group, n_q):
    return jnp.concatenate(
        [jnp.full((1, n_q), sink_ref[g * group + j], F32) for j in range(group)], axis=1)


def _attn_a_kernel(sink_ref, qt_ref, k0_ref, k1_ref, k2_ref, k3_ref, v0_ref, v1_ref, v2_ref, v3_ref,
                   kc_ref, vc_ref, o_ref, st_ref):
    t = pl.program_id(1)
    nt = pl.num_programs(1)
    tq = o_ref.shape[1]
    n_win = 4 * WIN_BLOCK
    group = A_Q_HEADS // A_KV_HEADS
    kj = lax.broadcasted_iota(jnp.int32, (n_win, tq), 0)
    qi = lax.broadcasted_iota(jnp.int32, (n_win, tq), 1)
    rel = kj - WIN_BLOCK - qi
    valid = (rel <= WINDOW) & (rel >= -WINDOW)
    valid = valid & ((kj >= WIN_BLOCK) | (t > 0)) & ((kj < 3 * WIN_BLOCK) | (t < nt - 1))
    mask = jnp.where(valid, 0.0, NEG_INF).astype(F32)
    mask = jnp.concatenate([mask] * group, axis=1)

    def scores_to(g):
        keys = jnp.concatenate([r[0, g] for r in (k0_ref, k1_ref, k2_ref, k3_ref, kc_ref)], axis=0)
        st = _dot(keys, _stack_heads_t(qt_ref, g, group))
        st_ref[g, 0:n_win] = st[0:n_win] + mask
        st_ref[g, n_win:] = st[n_win:]

    def attend(g):
        vals_t = jnp.concatenate([r[0, g] for r in (v0_ref, v1_ref, v2_ref, v3_ref, vc_ref)], axis=1)
        og = _softmax_pv_t(st_ref[g], vals_t, sink=_sink_row(sink_ref, g, group, tq))
        return [og[:, j * tq:(j + 1) * tq] for j in range(group)]

    heads_t = []
    scores_to(0)
    for g in range(A_KV_HEADS):
        if g + 1 < A_KV_HEADS:
            scores_to(g + 1)
        heads_t += attend(g)
    o_ref[0] = jnp.concatenate(heads_t, axis=0).T.astype(BF16)


def _attention_a(sink, qt, k, vt, k_ctx, vt_ctx):
    bsz, _, n_tok = qt.shape
    tq = ROW_TILE
    nb = n_tok // WIN_BLOCK
    per = tq // WIN_BLOCK
    slot = lambda s: (lambda b, i: jnp.clip(per * i - 1 + s, 0, nb - 1))
    k_spec = lambda s: pl.BlockSpec((1, A_KV_HEADS, WIN_BLOCK, HEAD_DIM),
                                    lambda b, i: (b, 0, slot(s)(b, i), 0))
    v_spec = lambda s: pl.BlockSpec((1, A_KV_HEADS, 2 * HEAD_DIM, WIN_BLOCK),
                                    lambda b, i: (b, 0, 0, slot(s)(b, i)))
    return pl.pallas_call(
        _attn_a_kernel,
        grid=(bsz, n_tok // tq),
        in_specs=[pl.BlockSpec(memory_space=pltpu.SMEM),
                  pl.BlockSpec((1, Q_WIDTH, tq), lambda b, i: (b, 0, i))]
                 + [k_spec(s) for s in range(4)] + [v_spec(s) for s in range(4)]
                 + [pl.BlockSpec((1, A_KV_HEADS, CTX_LEN, HEAD_DIM), lambda b, i: (b, 0, 0, 0)),
                    pl.BlockSpec((1, A_KV_HEADS, 2 * HEAD_DIM, CTX_LEN), lambda b, i: (b, 0, 0, 0))],
        out_specs=pl.BlockSpec((1, tq, Q_WIDTH), lambda b, i: (b, i, 0)),
        out_shape=jax.ShapeDtypeStruct((bsz, n_tok, Q_WIDTH), BF16),
        scratch_shapes=[pltpu.VMEM((A_KV_HEADS, 4 * WIN_BLOCK + CTX_LEN,
                                    (A_Q_HEADS // A_KV_HEADS) * tq), F32)],
        compiler_params=_params(("arbitrary", "arbitrary")),
        name="attn_window",
    )(sink, qt, k, k, k, k, vt, vt, vt, vt, k_ctx, vt_ctx)


def _attn_b_kernel(bound_ref, qt_ref, kc_ref, vtc_ref, k_ref, vt_ref, o_ref, st0_ref, st1_ref):
    st_ref = (st0_ref, st1_ref)
    tq = o_ref.shape[1]
    n_chunks = k_ref.shape[2] // KV_CHUNK
    group = B_Q_HEADS // B_KV_HEADS
    groups = range(B_KV_HEADS)
    qts = [_stack_heads_t(qt_ref, g, group) for g in groups]

    def chunk(c, g):
        start = pl.multiple_of(c * KV_CHUNK, KV_CHUNK)
        return k_ref[0, g, pl.ds(start, KV_CHUNK), :], vt_ref[0, g, :, pl.ds(start, KV_CHUNK)]

    def write(outs_t):
        heads_t = [og[:, j * tq:(j + 1) * tq] for og in outs_t for j in range(group)]
        o_ref[0] = jnp.concatenate(heads_t, axis=0).T.astype(BF16)

    bound = bound_ref[0]
    bounded = bound <= MAX_SAFE_SCORE_BOUND

    @pl.when(bounded)
    def _():
        def scores_to(slot, c):
            for g in groups:
                st_ref[slot][g] = _dot(chunk(c, g)[0], qts[g])

        def accumulate(accs, slot, c):
            return tuple(accs[g] + _dot(chunk(c, g)[1],
                                        jnp.exp2(st_ref[slot][g] - bound).astype(BF16))
                         for g in groups)

        def stage(accs, cur, c):
            out = []
            for g in groups:
                k_next = chunk(c + 1, g)[0]
                vt_cur = chunk(c, g)[1]
                parts = []
                for j in range(group):
                    ls = slice(j * tq, (j + 1) * tq)
                    st_ref[1 - cur][g, :, ls] = _dot(k_next, qts[g][:, ls])
                    p = jnp.exp2(st_ref[cur][g, :, ls] - bound).astype(BF16)
                    parts.append(accs[g][:, ls] + _dot(vt_cur, p))
                out.append(jnp.concatenate(parts, axis=1))
            return tuple(out)

        def body(i, accs):
            c = 2 * i
            return stage(stage(accs, 0, c), 1, c + 1)

        scores_to(0, 0)
        accs = tuple(_bounded_update_t(jnp.zeros((2 * HEAD_DIM, group * tq), F32),
                                       kc_ref[0, g], vtc_ref[0, g], qts[g], bound) for g in groups)
        accs = lax.fori_loop(0, n_chunks // 2 - 1, body, accs)
        accs = stage(accs, 0, n_chunks - 2)
        accs = accumulate(accs, 1, n_chunks - 1)
        write([acc[0:HEAD_DIM] / acc[HEAD_DIM:HEAD_DIM + 1] for acc in accs])

    @pl.when(jnp.logical_not(bounded))
    def _():
        carries = tuple(_flash_update_t(_flash_init_t(group * tq), kc_ref[0, g], vtc_ref[0, g], qts[g])
                        for g in groups)
        carries = lax.fori_loop(
            0, n_chunks,
            lambda c, carries: tuple(_flash_update_t(carries[g], *chunk(c, g), qts[g])
                                     for g in groups), carries)
        write([_flash_finish_t(carry) for carry in carries])


def _attention_b(bound, qt, k, vt, k_ctx, vt_ctx):
    bsz, _, n_tok = qt.shape
    tq = GLOBAL_Q_TILE
    keys = lambda n: pl.BlockSpec((1, B_KV_HEADS, n, HEAD_DIM), lambda b, i: (b, 0, 0, 0))
    vals = lambda n: pl.BlockSpec((1, B_KV_HEADS, 2 * HEAD_DIM, n), lambda b, i: (b, 0, 0, 0))
    return pl.pallas_call(
        _attn_b_kernel,
        grid=(bsz, n_tok // tq),
        in_specs=[pl.BlockSpec(memory_space=pltpu.SMEM),
                  pl.BlockSpec((1, Q_WIDTH, tq), lambda b, i: (b, 0, i)),
                  keys(CTX_LEN), vals(CTX_LEN), keys(n_tok), vals(n_tok)],
        out_specs=pl.BlockSpec((1, tq, Q_WIDTH), lambda b, i: (b, i, 0)),
        out_shape=jax.ShapeDtypeStruct((bsz, n_tok, Q_WIDTH), BF16),
        scratch_shapes=[pltpu.VMEM((B_KV_HEADS, KV_CHUNK, (B_Q_HEADS // B_KV_HEADS) * tq), F32)] * 2,
        compiler_params=_params(("arbitrary", "arbitrary")),
        name="attn_global",
    )(bound, qt, k_ctx, vt_ctx, k, vt)


def _attn_c_kernel(tab_ref, qt_ref, k0_ref, k1_ref, k2_ref, v0_ref, v1_ref, v2_ref,
                   kc_ref, vc_ref, o_ref, st_ref):
    n_win = tab_ref.shape[2]

    def scores_to(h):
        keys = jnp.concatenate([r[0, h] for r in (k0_ref, k1_ref, k2_ref, kc_ref)], axis=0)
        st = _dot(keys, qt_ref[0, h * HEAD_DIM:(h + 1) * HEAD_DIM, :])
        st_ref[h % 2, 0:n_win] = st[0:n_win] + tab_ref[0, h]
        st_ref[h % 2, n_win:] = st[n_win:]

    def attend(h):
        vals_t = jnp.concatenate([r[0, h] for r in (v0_ref, v1_ref, v2_ref, vc_ref)], axis=1)
        return _softmax_pv_t(st_ref[h % 2], vals_t)

    heads_t = []
    scores_to(0)
    for h in range(C_HEADS):
        if h + 1 < C_HEADS:
            scores_to(h + 1)
        heads_t.append(attend(h))
    o_ref[0] = jnp.concatenate(heads_t, axis=0).T.astype(BF16)


def _attention_c(tab, qt, k, vt, k_ctx, vt_ctx):
    bsz, _, n_tok = qt.shape
    tq = ROW_TILE
    nt = n_tok // tq
    slot = lambda s: (lambda b, i: jnp.clip(i - 1 + s, 0, nt - 1))
    k_spec = lambda s: pl.BlockSpec((1, C_HEADS, tq, HEAD_DIM), lambda b, i: (b, 0, slot(s)(b, i), 0))
    v_spec = lambda s: pl.BlockSpec((1, C_HEADS, 2 * HEAD_DIM, tq),
                                    lambda b, i: (b, 0, 0, slot(s)(b, i)))
    variant = lambda b, i: (jnp.where(i == 0, 0, jnp.where(i == nt - 1, 2, 1)), 0, 0, 0)
    return pl.pallas_call(
        _attn_c_kernel,
        grid=(bsz, nt),
        in_specs=[pl.BlockSpec((1, C_HEADS, 3 * tq, tq), variant),
                  pl.BlockSpec((1, Q_WIDTH, tq), lambda b, i: (b, 0, i))]
                 + [k_spec(s) for s in range(3)] + [v_spec(s) for s in range(3)]
                 + [pl.BlockSpec((1, C_HEADS, CTX_LEN, HEAD_DIM), lambda b, i: (b, 0, 0, 0)),
                    pl.BlockSpec((1, C_HEADS, 2 * HEAD_DIM, CTX_LEN), lambda b, i: (b, 0, 0, 0))],
        out_specs=pl.BlockSpec((1, tq, Q_WIDTH), lambda b, i: (b, i, 0)),
        out_shape=jax.ShapeDtypeStruct((bsz, n_tok, Q_WIDTH), BF16),
        scratch_shapes=[pltpu.VMEM((2, 3 * tq + CTX_LEN, tq), F32)],
        compiler_params=_params(("arbitrary", "arbitrary")),
        name="attn_nbr",
    )(tab, qt, k, k, k, vt, vt, vt, k_ctx, vt_ctx)


def _na_table_kernel(rowmat_ref, o_ref):
    rows = SEQ // GRID_W
    rpt = ROW_TILE // GRID_W
    nt = rows // rpt
    kh = min(NA_KH_MAX, rows)
    masked = jnp.full((GRID_W, GRID_W), NEG_INF, F32)
    for v, t in enumerate((0, 1, nt - 1)):
        for rl in range(rpt):
            r = rpt * t + rl
            rs = min(max(r - kh // 2, 0), rows - kh)
            for sk in range(3 * rpt):
                kr = rpt * (t - 1) + sk
                blk = rowmat_ref[0, kr - r + NA_KH_MAX - 1] if rs <= kr < rs + kh else masked
                o_ref[v, 0, sk * GRID_W:(sk + 1) * GRID_W, rl * GRID_W:(rl + 1) * GRID_W] = blk


def _neighbourhood_tables(rpb):
    c = np.arange(GRID_W)
    ws = np.clip(c - NA_KW // 2, 0, GRID_W - NA_KW)
    valid = (c[None, :] >= ws[:, None]) & (c[None, :] < ws[:, None] + NA_KW)
    coff = np.clip(c[None, :] - c[:, None] + NA_KW - 1, 0, 2 * NA_KW - 2)
    onehot = (coff[:, :, None] == np.arange(2 * NA_KW - 1)).astype(np.float32)
    rowmat = jnp.einsum("hij,ckj->hikc", rpb * LOG2_E, onehot, precision=lax.Precision.HIGHEST)
    rowmat = jnp.where(valid.T, rowmat, NEG_INF).astype(F32)
    n_off = 2 * NA_KH_MAX - 1
    return pl.pallas_call(
        _na_table_kernel,
        grid=(C_HEADS,),
        in_specs=[pl.BlockSpec((1, n_off, GRID_W, GRID_W), lambda h: (h, 0, 0, 0))],
        out_specs=pl.BlockSpec((3, 1, 3 * ROW_TILE, ROW_TILE), lambda h: (0, h, 0, 0)),
        out_shape=jax.ShapeDtypeStruct((3, C_HEADS, 3 * ROW_TILE, ROW_TILE), F32),
        compiler_params=_params(("arbitrary",)),
        name="na_tables",
    )(rowmat)


def _attn_ctx_kernel(sink_ref, qa_ref, ka_ref, va_ref, qb_ref, kb_ref, vb_ref,
                     qc_ref, kc_ref, vc_ref, oa_ref, ob_ref, oc_ref):
    group = A_Q_HEADS // A_KV_HEADS
    n_q = oa_ref.shape[1]
    outs = ([], [], [])
    for h in range(A_Q_HEADS):
        g = h // group
        hs = slice(h * HEAD_DIM, (h + 1) * HEAD_DIM)
        init = _flash_init_t(n_q)
        outs[0].append(_flash_finish_t(
            _flash_update_t(init, ka_ref[0, g], va_ref[0, g], qa_ref[0, hs, :]),
            sink=jnp.full((1, n_q), sink_ref[h], F32)))
        outs[1].append(_flash_finish_t(
            _flash_update_t(init, kb_ref[0, g], vb_ref[0, g], qb_ref[0, hs, :])))
        outs[2].append(_flash_finish_t(
            _flash_update_t(init, kc_ref[0, h], vc_ref[0, h], qc_ref[0, hs, :])))
    for o_ref, heads_t in zip((oa_ref, ob_ref, oc_ref), outs):
        o_ref[0] = jnp.concatenate(heads_t, axis=0).T.astype(BF16)


def _attention_ctx(sink, qa, ka, va, qb, kb, vb, qc, kc, vc):
    bsz = qa.shape[0]
    q_spec = pl.BlockSpec((1, Q_WIDTH, CTX_LEN), lambda b: (b, 0, 0))
    k_spec = lambda n: pl.BlockSpec((1, n, CTX_LEN, HEAD_DIM), lambda b: (b, 0, 0, 0))
    v_spec = lambda n: pl.BlockSpec((1, n, 2 * HEAD_DIM, CTX_LEN), lambda b: (b, 0, 0, 0))
    out = jax.ShapeDtypeStruct((bsz, CTX_LEN, Q_WIDTH), BF16)
    specs = [pl.BlockSpec(memory_space=pltpu.SMEM)]
    for n in (A_KV_HEADS, B_KV_HEADS, C_HEADS):
        specs += [q_spec, k_spec(n), v_spec(n)]
    return pl.pallas_call(
        _attn_ctx_kernel,
        grid=(bsz,),
        in_specs=specs,
        out_specs=[pl.BlockSpec((1, CTX_LEN, Q_WIDTH), lambda b: (b, 0, 0))] * 3,
        out_shape=[out] * 3,
        compiler_params=_params(("arbitrary",)),
        name="attn_ctx",
    )(sink, qa, ka, va, qb, kb, vb, qc, kc, vc)


def _post_kernel(x_ref, oa_ref, ob_ref, oc_ref, sg_ref, mod_ref, norms_ref,
                 wa_ref, wb_ref, wc_ref, wo_ref, w1_ref, w2_ref, o_ref):
    x = x_ref[0]
    merged = None
    for j, (o_r, w_r) in enumerate(((oa_ref, wa_ref), (ob_ref, wb_ref), (oc_ref, wc_ref))):
        gate = sg_ref[0, :, j * D_MODEL:(j + 1) * D_MODEL].astype(F32)
        term = gate * _dot(o_r[0], w_r[...])
        merged = term if merged is None else merged + term
    y = _dot(merged.astype(BF16), wo_ref[...])
    x1 = x + mod_ref[0, 2:3, :] * (_rms(y) * norms_ref[0:1, :])
    h2 = _rms(x1) * norms_ref[1:2, :]
    h2 = (h2 * (1.0 + mod_ref[0, 4:5, :]) + mod_ref[0, 3:4, :]).astype(BF16)
    z = None
    n_chunk = MLP_HIDDEN // D_MODEL
    for j in range(n_chunk):
        cs = slice(j * D_MODEL, (j + 1) * D_MODEL)
        a = jnp.maximum(_dot(h2, w1_ref[:, cs]), 0.0)
        zj = _dot((a * a).astype(BF16), w2_ref[cs, :])
        z = zj if z is None else z + zj
    o_ref[0] = x1 + mod_ref[0, 5:6, :] * (_rms(z) * norms_ref[2:3, :])


def _post_block(x, o_a, o_b, o_c, sg, mod, norms, wa, wb, wc, wo, w1, w2):
    bsz, n_tok, _ = x.shape
    tm = ROW_TILE
    row = lambda w: pl.BlockSpec((1, tm, w), lambda b, i: (b, i, 0))
    return pl.pallas_call(
        _post_kernel,
        grid=(bsz, n_tok // tm),
        in_specs=[row(D_MODEL), row(Q_WIDTH), row(Q_WIDTH), row(Q_WIDTH), row(3 * D_MODEL),
                  pl.BlockSpec((1, 6, D_MODEL), lambda b, i: (b, 0, 0)),
                  _const_spec((8, D_MODEL)),
                  _const_spec((Q_WIDTH, D_MODEL)), _const_spec((Q_WIDTH, D_MODEL)),
                  _const_spec((Q_WIDTH, D_MODEL)), _const_spec((D_MODEL, D_MODEL)),
                  _const_spec((D_MODEL, MLP_HIDDEN)), _const_spec((MLP_HIDDEN, D_MODEL))],
        out_specs=row(D_MODEL),
        out_shape=jax.ShapeDtypeStruct((bsz, n_tok, D_MODEL), F32),
        compiler_params=_params(("arbitrary", "arbitrary")),
        name="post_mlp",
    )(x, o_a, o_b, o_c, sg, mod, norms, wa, wb, wc, wo, w1, w2)


def _rope_tables(n_tok):
    pos = jnp.arange(n_tok)
    row = (pos // GRID_W).astype(F32)
    col = (pos % GRID_W).astype(F32)
    n_freq = HEAD_DIM // 4
    freqs = ROPE_THETA ** (-jnp.arange(n_freq, dtype=F32) / n_freq)
    ang = jnp.concatenate([row[:, None] * freqs, col[:, None] * freqs], axis=-1)
    cos = jnp.repeat(jnp.cos(ang), 2, axis=-1)
    sign = jnp.tile(jnp.array([-1.0, 1.0], F32), HEAD_DIM // 2)
    sin = jnp.repeat(jnp.sin(ang), 2, axis=-1) * sign
    return jnp.tile(cos, (1, LANES // HEAD_DIM)), jnp.tile(sin, (1, LANES // HEAD_DIM))


def _extended_w_in(w_in_l):
    sizes = (Q_WIDTH, KV_WIDTH, KV_WIDTH, Q_WIDTH, KV_WIDTH, KV_WIDTH, Q_WIDTH, Q_WIDTH, Q_WIDTH)
    parts, start = [], 0
    for n in sizes:
        parts.append(w_in_l[:, start:start + n])
        start += n
    qa, ka, va, qb, kb, vb, qc, kc, vc = parts
    gates = w_in_l[:, start:]
    swap = lambda w: w[:, jnp.arange(w.shape[1]) ^ 1]
    cols = [qa * Q_SCALE, ka, swap(qa) * Q_SCALE, swap(ka), va,
            qb, kb, swap(qb), swap(kb), vb,
            qc * Q_SCALE, kc, vc, gates]
    return jnp.concatenate(cols, axis=1).astype(BF16)


def kernel(x, c, ctx, c_ctx, w_ada, b_ada, norm_mix_pre, norm_mix_post, w_in, sink_a, qnorm_b,
           knorm_b, rpb_c, w_br_a, w_br_b, w_br_c, w_out, norm_mlp_pre, norm_mlp_post,
           w_mlp_in, w_mlp_out):
    bsz, seq, _ = x.shape
    n_ctx = ctx.shape[1]
    cvec = jnp.concatenate([c, c_ctx[None, :], jnp.zeros((8 - bsz - 1, D_MODEL), F32)], axis=0)
    mod_all = _ada_modulation(cvec, w_ada, b_ada)

    cos_lat, sin_lat = _rope_tables(seq)
    cos_ctx = jnp.ones((n_ctx, LANES), F32)
    sin_ctx = jnp.zeros((n_ctx, LANES), F32)
    head_id = jnp.arange(QK_W) // HEAD_DIM
    gmat = (head_id[:, None] == head_id[None, :]).astype(BF16)
    tile2 = lambda g: jnp.tile(g, LANES // HEAD_DIM)
    swap1 = lambda g: g[jnp.arange(HEAD_DIM) ^ 1]

    x_lat, x_ctx = x, ctx
    for l in range(DEPTH):
        last = l == DEPTH - 1
        mod_lat = mod_all[l, :bsz].reshape(bsz, 6, D_MODEL)
        mod_ctx = jnp.broadcast_to(mod_all[l, bsz].reshape(1, 6, D_MODEL), (bsz, 6, D_MODEL))
        gq, gk = qnorm_b[l], knorm_b[l]
        vecs = jnp.stack([tile2(gq) * Q_SCALE, tile2(swap1(gq)) * Q_SCALE,
                          tile2(gk), tile2(swap1(gk))]
                         + [jnp.zeros((LANES,), F32)] * 4)
        score_bound = (1.02 * HEAD_DIM * Q_SCALE * jnp.max(jnp.abs(gq)) * jnp.max(jnp.abs(gk))
                       ).reshape(1).astype(F32)
        gpre = norm_mix_pre[l][None, :]
        w_ext = _extended_w_in(w_in[l])
        norms = jnp.stack([norm_mix_post[l], norm_mlp_pre[l], norm_mlp_post[l]]
                          + [jnp.zeros((D_MODEL,), F32)] * 5)
        wa, wb, wc = (w.astype(BF16) for w in (w_br_a[l], w_br_b[l], w_br_c[l]))
        wo = w_out[l].astype(BF16)
        w1 = w_mlp_in[l].astype(BF16)
        w2 = w_mlp_out[l].astype(BF16)
        tabs = _neighbourhood_tables(rpb_c[l])
        sink = sink_a[l] * LOG2_E

        proj_lat = _input_projection(x_lat, mod_lat, gpre, vecs, cos_lat, sin_lat, gmat, w_ext)
        proj_ctx = _input_projection(x_ctx, mod_ctx, gpre, vecs, cos_ctx, sin_ctx, gmat, w_ext)
        qa, ka, va, qb, kb, vb, qc, kc, vc, sg = proj_lat
        qa_c, ka_c, va_c, qb_c, kb_c, vb_c, qc_c, kc_c, vc_c, sg_c = proj_ctx

        o_a = _attention_a(sink, qa, ka, va, ka_c, va_c)
        o_b = _attention_b(score_bound, qb, kb, vb, kb_c, vb_c)
        o_c = _attention_c(tabs, qc, kc, vc, kc_c, vc_c)
        x_lat = _post_block(x_lat, o_a, o_b, o_c, sg, mod_lat, norms, wa, wb, wc, wo, w1, w2)
        if not last:
            o_a_c, o_b_c, o_c_c = _attention_ctx(sink, qa_c, ka_c, va_c, qb_c, kb_c, vb_c,
                                                 qc_c, kc_c, vc_c)
            x_ctx = _post_block(x_ctx, o_a_c, o_b_c, o_c_c, sg_c, mod_ctx, norms,
                                wa, wb, wc, wo, w1, w2)
    return x_lat
```

```python
import jax
import jax.numpy as jnp
import numpy as np
from jax import lax
from jax.experimental import pallas as pl
from jax.experimental.pallas import tpu as pltpu

D_MODEL = 1024
SEQ = 8192
DEPTH = 2
GRID_W = 64
CTX_LEN = 256
HEAD_DIM = 64
A_Q_HEADS = 6
A_KV_HEADS = 2
WINDOW = 128
B_Q_HEADS = 6
B_KV_HEADS = 2
C_HEADS = 6
NA_KH_MAX = 8
NA_KW = 16
Q_WIDTH = 6 * HEAD_DIM
KV_WIDTH = 2 * HEAD_DIM
MLP_HIDDEN = 4 * D_MODEL
ROPE_THETA = 10000.0
NORM_EPS = 1e-6
NEG_INF = -1e30
SM_SCALE = HEAD_DIM ** -0.5
LOG2_E = 1.4426950408889634
Q_SCALE = SM_SCALE * LOG2_E

V7X_VMEM_LIMIT_BYTES = 56 * 1024 * 1024
LANES = 128
ROW_TILE = 256
DENSE_ROW_TILE = 512
WIN_BLOCK = 128
KV_CHUNK = 512
GLOBAL_Q_TILE = 512
ADA_COLS = 1536
MAX_SAFE_SCORE_BOUND = 60.0

BF16 = jnp.bfloat16
F32 = jnp.float32

A_OFF = 0
B_OFF = A_OFF + 640
C_OFF = B_OFF + 640
G_OFF = C_OFF + 1152
EXT_COLS = G_OFF + 3 * D_MODEL
QK_W = Q_WIDTH + KV_WIDTH


def _params(semantics):
    return pltpu.CompilerParams(dimension_semantics=semantics,
                                vmem_limit_bytes=V7X_VMEM_LIMIT_BYTES)


def _const_spec(shape):
    nd = len(shape)
    return pl.BlockSpec(shape, lambda *_: (0,) * nd, pipeline_mode=pl.Buffered(1))


def _dot(a, b):
    return jnp.dot(a, b, preferred_element_type=F32)


def _rms(x):
    return x * lax.rsqrt(jnp.mean(x * x, axis=-1, keepdims=True) + NORM_EPS)


def _ada_kernel(c_ref, w_ref, b_ref, o_ref):
    c = c_ref[...]
    a = c / (1.0 + jnp.exp(-c))
    w = w_ref[0]
    a_hi = a.astype(BF16)
    a_lo = (a - a_hi.astype(F32)).astype(BF16)
    w_hi = w.astype(BF16)
    w_lo = (w - w_hi.astype(F32)).astype(BF16)
    acc = _dot(a_hi, w_hi) + _dot(a_hi, w_lo) + _dot(a_lo, w_hi)
    o_ref[0] = acc + b_ref[0]


def _ada_modulation(cvec, w_ada, b_ada):
    n_col = 6 * D_MODEL
    return pl.pallas_call(
        _ada_kernel,
        grid=(DEPTH, n_col // ADA_COLS),
        in_specs=[
            pl.BlockSpec((8, D_MODEL), lambda l, j: (0, 0)),
            pl.BlockSpec((1, D_MODEL, ADA_COLS), lambda l, j: (l, 0, j)),
            pl.BlockSpec((1, 1, ADA_COLS), lambda l, j: (l, 0, j)),
        ],
        out_specs=pl.BlockSpec((1, 8, ADA_COLS), lambda l, j: (l, 0, j)),
        out_shape=jax.ShapeDtypeStruct((DEPTH, 8, n_col), F32),
        compiler_params=_params(("arbitrary", "arbitrary")),
        name="ada_mod",
    )(cvec, w_ada, b_ada.reshape(DEPTH, 1, n_col))


def _store_heads(k_ref, k, n_heads):
    for h in range(n_heads):
        k_ref[0, h] = k[:, h * HEAD_DIM:(h + 1) * HEAD_DIM].astype(BF16)


def _store_values_t(vt_ref, v, n_heads):
    vt = v.T.astype(BF16)
    pad_row = lax.broadcasted_iota(jnp.int32, (HEAD_DIM, v.shape[0]), 0)
    ones_pad = jnp.where(pad_row == 0, 1.0, 0.0).astype(BF16)
    for h in range(n_heads):
        vt_ref[0, h, 0:HEAD_DIM, :] = vt[h * HEAD_DIM:(h + 1) * HEAD_DIM]
        vt_ref[0, h, HEAD_DIM:2 * HEAD_DIM, :] = ones_pad


def _pair_swap(x):
    lane = lax.broadcasted_iota(jnp.int32, x.shape, 1)
    return jnp.where((lane & 1) == 0, pltpu.roll(x, LANES - 1, axis=1), pltpu.roll(x, 1, axis=1))


def _proj_kernel(x_ref, mod_ref, gpre_ref, vecs_ref, cos_ref, sin_ref, gmat_ref, w_ref,
                 qa_ref, ka_ref, va_ref, qb_ref, kb_ref, vb_ref, qc_ref, kc_ref, vc_ref, sg_ref):
    x = x_ref[0]
    h = _rms(x) * gpre_ref[...]
    h = h * (1.0 + mod_ref[0, 1:2, :]) + mod_ref[0, 0:1, :]
    hb = h.astype(BF16)
    cos = cos_ref[...]
    sin = sin_ref[...]

    ra = _dot(hb, w_ref[:, A_OFF:A_OFF + QK_W + KV_WIDTH])
    for j in range(4):
        lo = j * LANES
        xa = ra[:, lo:lo + LANES]
        roped = xa * cos + _pair_swap(xa) * sin
        if j < 3:
            qa_ref[0, lo:lo + LANES, :] = roped.T.astype(BF16)
        else:
            _store_heads(ka_ref, roped, A_KV_HEADS)
    _store_values_t(va_ref, ra[:, QK_W:QK_W + KV_WIDTH], A_KV_HEADS)

    rb = _dot(hb, w_ref[:, B_OFF:B_OFF + QK_W + KV_WIDTH])
    xb = rb[:, 0:QK_W]
    sq = xb * xb
    sq_hi = sq.astype(BF16)
    sq_lo = (sq - sq_hi.astype(F32)).astype(BF16)
    head_ms = (_dot(sq_hi, gmat_ref[...]) + _dot(sq_lo, gmat_ref[...])) * (1.0 / HEAD_DIM)
    rinv = lax.rsqrt(head_ms + NORM_EPS)
    cq, sq_t = cos * vecs_ref[0:1, :], sin * vecs_ref[1:2, :]
    ck, sk_t = cos * vecs_ref[2:3, :], sin * vecs_ref[3:4, :]
    for j in range(4):
        lo = j * LANES
        c_t, s_t = (cq, sq_t) if j < 3 else (ck, sk_t)
        xj = rb[:, lo:lo + LANES]
        roped = rinv[:, lo:lo + LANES] * (xj * c_t + _pair_swap(xj) * s_t)
        if j < 3:
            qb_ref[0, lo:lo + LANES, :] = roped.T.astype(BF16)
        else:
            _store_heads(kb_ref, roped, B_KV_HEADS)
    _store_values_t(vb_ref, rb[:, QK_W:QK_W + KV_WIDTH], B_KV_HEADS)

    rc = _dot(hb, w_ref[:, C_OFF:C_OFF + 1152])
    qc_ref[0] = rc[:, 0:Q_WIDTH].T.astype(BF16)
    _store_heads(kc_ref, rc[:, Q_WIDTH:2 * Q_WIDTH], C_HEADS)
    _store_values_t(vc_ref, rc[:, 2 * Q_WIDTH:3 * Q_WIDTH], C_HEADS)

    for j in range(3):
        lo = G_OFF + j * D_MODEL
        g = _dot(hb, w_ref[:, lo:lo + D_MODEL])
        sg_ref[0, :, j * D_MODEL:(j + 1) * D_MODEL] = (1.0 / (1.0 + jnp.exp(-g))).astype(BF16)


def _input_projection(x, mod, gpre, vecs, cos_t, sin_t, gmat, w_ext):
    bsz, n_tok, _ = x.shape
    tm = ROW_TILE
    bf = lambda *s: jax.ShapeDtypeStruct(s, BF16)
    q_spec = pl.BlockSpec((1, Q_WIDTH, tm), lambda b, i: (b, 0, i))
    k_spec = lambda n: pl.BlockSpec((1, n, tm, HEAD_DIM), lambda b, i: (b, 0, i, 0))
    v_spec = lambda n: pl.BlockSpec((1, n, 2 * HEAD_DIM, tm), lambda b, i: (b, 0, 0, i))
    qkv_specs, qkv_shapes = [], []
    for n in (A_KV_HEADS, B_KV_HEADS, C_HEADS):
        qkv_specs += [q_spec, k_spec(n), v_spec(n)]
        qkv_shapes += [bf(bsz, Q_WIDTH, n_tok), bf(bsz, n, n_tok, HEAD_DIM),
                       bf(bsz, n, 2 * HEAD_DIM, n_tok)]
    return pl.pallas_call(
        _proj_kernel,
        grid=(bsz, n_tok // tm),
        in_specs=[
            pl.BlockSpec((1, tm, D_MODEL), lambda b, i: (b, i, 0)),
            pl.BlockSpec((1, 6, D_MODEL), lambda b, i: (b, 0, 0)),
            _const_spec((1, D_MODEL)),
            _const_spec((8, LANES)),
            pl.BlockSpec((tm, LANES), lambda b, i: (i, 0)),
            pl.BlockSpec((tm, LANES), lambda b, i: (i, 0)),
            _const_spec((QK_W, QK_W)),
            _const_spec((D_MODEL, EXT_COLS)),
        ],
        out_specs=qkv_specs + [pl.BlockSpec((1, tm, 3 * D_MODEL), lambda b, i: (b, i, 0))],
        out_shape=qkv_shapes + [bf(bsz, n_tok, 3 * D_MODEL)],
        compiler_params=_params(("arbitrary", "arbitrary")),
        name="in_proj",
    )(x, mod, gpre, vecs, cos_t, sin_t, gmat, w_ext)


def _flash_update_t(carry, k, vt, qt, bias=None):
    m, acc = carry
    st = _dot(k, qt)
    if bias is not None:
        st = st + bias
    m_new = jnp.maximum(m, jnp.max(st, axis=0, keepdims=True))
    alpha = jnp.exp2(m - m_new)
    p = jnp.exp2(st - m_new).astype(BF16)
    return m_new, alpha * acc + _dot(vt, p)


def _bounded_update_t(acc, k, vt, qt, bound):
    p = jnp.exp2(_dot(k, qt) - bound).astype(BF16)
    return acc + _dot(vt, p)


def _flash_init_t(n_q):
    return jnp.full((1, n_q), NEG_INF, F32), jnp.zeros((2 * HEAD_DIM, n_q), F32)


def _flash_finish_t(carry, sink=None):
    m, acc = carry
    denom = acc[HEAD_DIM:HEAD_DIM + 1]
    if sink is not None:
        denom = denom + jnp.exp2(sink - m)
    return acc[0:HEAD_DIM] / denom


def _softmax_pv_t(st, vt, sink=None):
    m = jnp.max(st, axis=0, keepdims=True)
    acc = _dot(vt, jnp.exp2(st - m).astype(BF16))
    denom = acc[HEAD_DIM:HEAD_DIM + 1]
    if sink is not None:
        denom = denom + jnp.exp2(sink - m)
    return acc[0:HEAD_DIM] / denom


def _stack_heads_t(qt_ref, g, group):
    return jnp.concatenate(
        [qt_ref[0, (g * group + j) * HEAD_DIM:(g * group + j + 1) * HEAD_DIM, :]
         for j in range(group)], axis=1)


def _sink_row(sink_ref, g, group, n_q):
    return jnp.concatenate(
        [jnp.full((1, n_q), sink_ref[g * group + j], F32) for j in range(group)], axis=1)


def _attn_a_kernel(sink_ref, qt_ref, k0_ref, k1_ref, k2_ref, k3_ref, v0_ref, v1_ref, v2_ref, v3_ref,
                   kc_ref, vc_ref, o_ref, st_ref):
    t = pl.program_id(1)
    nt = pl.num_programs(1)
    tq = o_ref.shape[1]
    n_win = 4 * WIN_BLOCK
    group = A_Q_HEADS // A_KV_HEADS
    kj = lax.broadcasted_iota(jnp.int32, (n_win, tq), 0)
    qi = lax.broadcasted_iota(jnp.int32, (n_win, tq), 1)
    rel = kj - WIN_BLOCK - qi
    valid = (rel <= WINDOW) & (rel >= -WINDOW)
    valid = valid & ((kj >= WIN_BLOCK) | (t > 0)) & ((kj < 3 * WIN_BLOCK) | (t < nt - 1))
    mask = jnp.where(valid, 0.0, NEG_INF).astype(F32)
    mask = jnp.concatenate([mask] * group, axis=1)

    def scores_to(g):
        keys = jnp.concatenate([r[0, g] for r in (k0_ref, k1_ref, k2_ref, k3_ref, kc_ref)], axis=0)
        st = _dot(keys, _stack_heads_t(qt_ref, g, group))
        st_ref[g, 0:n_win] = st[0:n_win] + mask
        st_ref[g, n_win:] = st[n_win:]

    def attend(g):
        vals_t = jnp.concatenate([r[0, g] for r in (v0_ref, v1_ref, v2_ref, v3_ref, vc_ref)], axis=1)
        og = _softmax_pv_t(st_ref[g], vals_t, sink=_sink_row(sink_ref, g, group, tq))
        return [og[:, j * tq:(j + 1) * tq] for j in range(group)]

    heads_t = []
    scores_to(0)
    for g in range(A_KV_HEADS):
        if g + 1 < A_KV_HEADS:
            scores_to(g + 1)
        heads_t += attend(g)
    o_ref[0] = jnp.concatenate(heads_t, axis=0).T.astype(BF16)


def _attention_a(sink, qt, k, vt, k_ctx, vt_ctx):
    bsz, _, n_tok = qt.shape
    tq = ROW_TILE
    nb = n_tok // WIN_BLOCK
    per = tq // WIN_BLOCK
    slot = lambda s: (lambda b, i: jnp.clip(per * i - 1 + s, 0, nb - 1))
    k_spec = lambda s: pl.BlockSpec((1, A_KV_HEADS, WIN_BLOCK, HEAD_DIM),
                                    lambda b, i: (b, 0, slot(s)(b, i), 0))
    v_spec = lambda s: pl.BlockSpec((1, A_KV_HEADS, 2 * HEAD_DIM, WIN_BLOCK),
                                    lambda b, i: (b, 0, 0, slot(s)(b, i)))
    return pl.pallas_call(
        _attn_a_kernel,
        grid=(bsz, n_tok // tq),
        in_specs=[pl.BlockSpec(memory_space=pltpu.SMEM),
                  pl.BlockSpec((1, Q_WIDTH, tq), lambda b, i: (b, 0, i))]
                 + [k_spec(s) for s in range(4)] + [v_spec(s) for s in range(4)]
                 + [pl.BlockSpec((1, A_KV_HEADS, CTX_LEN, HEAD_DIM), lambda b, i: (b, 0, 0, 0)),
                    pl.BlockSpec((1, A_KV_HEADS, 2 * HEAD_DIM, CTX_LEN), lambda b, i: (b, 0, 0, 0))],
        out_specs=pl.BlockSpec((1, tq, Q_WIDTH), lambda b, i: (b, i, 0)),
        out_shape=jax.ShapeDtypeStruct((bsz, n_tok, Q_WIDTH), BF16),
        scratch_shapes=[pltpu.VMEM((A_KV_HEADS, 4 * WIN_BLOCK + CTX_LEN,
                                    (A_Q_HEADS // A_KV_HEADS) * tq), F32)],
        compiler_params=_params(("arbitrary", "arbitrary")),
        name="attn_window",
    )(sink, qt, k, k, k, k, vt, vt, vt, vt, k_ctx, vt_ctx)


def _attn_b_kernel(bound_ref, qt_ref, kc_ref, vtc_ref, k_ref, vt_ref, o_ref, st0_ref, st1_ref):
    st_ref = (st0_ref, st1_ref)
    tq = o_ref.shape[1]
    n_chunks = k_ref.shape[2] // KV_CHUNK
    group = B_Q_HEADS // B_KV_HEADS
    groups = range(B_KV_HEADS)
    qts = [_stack_heads_t(qt_ref, g, group) for g in groups]

    def chunk(c, g):
        start = pl.multiple_of(c * KV_CHUNK, KV_CHUNK)
        return k_ref[0, g, pl.ds(start, KV_CHUNK), :], vt_ref[0, g, :, pl.ds(start, KV_CHUNK)]

    def write(outs_t):
        heads_t = [og[:, j * tq:(j + 1) * tq] for og in outs_t for j in range(group)]
        o_ref[0] = jnp.concatenate(heads_t, axis=0).T.astype(BF16)

    bound = bound_ref[0]
    bounded = bound <= MAX_SAFE_SCORE_BOUND

    @pl.when(bounded)
    def _():
        def scores_to(slot, c):
            for g in groups:
                st_ref[slot][g] = _dot(chunk(c, g)[0], qts[g])

        def accumulate(accs, slot, c):
            return tuple(accs[g] + _dot(chunk(c, g)[1],
                                        jnp.exp2(st_ref[slot][g] - bound).astype(BF16))
                         for g in groups)

        def stage(accs, cur, c):
            out = []
            for g in groups:
                k_next = chunk(c + 1, g)[0]
                vt_cur = chunk(c, g)[1]
                parts = []
                for j in range(group):
                    ls = slice(j * tq, (j + 1) * tq)
                    st_ref[1 - cur][g, :, ls] = _dot(k_next, qts[g][:, ls])
                    p = jnp.exp2(st_ref[cur][g, :, ls] - bound).astype(BF16)
                    parts.append(accs[g][:, ls] + _dot(vt_cur, p))
                out.append(jnp.concatenate(parts, axis=1))
            return tuple(out)

        def body(i, accs):
            c = 2 * i
            return stage(stage(accs, 0, c), 1, c + 1)

        scores_to(0, 0)
        accs = tuple(_bounded_update_t(jnp.zeros((2 * HEAD_DIM, group * tq), F32),
                                       kc_ref[0, g], vtc_ref[0, g], qts[g], bound) for g in groups)
        accs = lax.fori_loop(0, n_chunks // 2 - 1, body, accs)
        accs = stage(accs, 0, n_chunks - 2)
        accs = accumulate(accs, 1, n_chunks - 1)
        write([acc[0:HEAD_DIM] / acc[HEAD_DIM:HEAD_DIM + 1] for acc in accs])

    @pl.when(jnp.logical_not(bounded))
    def _():
        carries = tuple(_flash_update_t(_flash_init_t(group * tq), kc_ref[0, g], vtc_ref[0, g], qts[g])
                        for g in groups)
        carries = lax.fori_loop(
            0, n_chunks,
            lambda c, carries: tuple(_flash_update_t(carries[g], *chunk(c, g), qts[g])
                                     for g in groups), carries)
        write([_flash_finish_t(carry) for carry in carries])


def _attention_b(bound, qt, k, vt, k_ctx, vt_ctx):
    bsz, _, n_tok = qt.shape
    tq = GLOBAL_Q_TILE
    keys = lambda n: pl.BlockSpec((1, B_KV_HEADS, n, HEAD_DIM), lambda b, i: (b, 0, 0, 0))
    vals = lambda n: pl.BlockSpec((1, B_KV_HEADS, 2 * HEAD_DIM, n), lambda b, i: (b, 0, 0, 0))
    return pl.pallas_call(
        _attn_b_kernel,
        grid=(bsz, n_tok // tq),
        in_specs=[pl.BlockSpec(memory_space=pltpu.SMEM),
                  pl.BlockSpec((1, Q_WIDTH, tq), lambda b, i: (b, 0, i)),
                  keys(CTX_LEN), vals(CTX_LEN), keys(n_tok), vals(n_tok)],
        out_specs=pl.BlockSpec((1, tq, Q_WIDTH), lambda b, i: (b, i, 0)),
        out_shape=jax.ShapeDtypeStruct((bsz, n_tok, Q_WIDTH), BF16),
        scratch_shapes=[pltpu.VMEM((B_KV_HEADS, KV_CHUNK, (B_Q_HEADS // B_KV_HEADS) * tq), F32)] * 2,
        compiler_params=_params(("arbitrary", "arbitrary")),
        name="attn_global",
    )(bound, qt, k_ctx, vt_ctx, k, vt)


def _attn_c_kernel(tab_ref, qt_ref, k0_ref, k1_ref, k2_ref, v0_ref, v1_ref, v2_ref,
                   kc_ref, vc_ref, o_ref, st_ref):
    n_win = tab_ref.shape[2]

    def scores_to(h):
        keys = jnp.concatenate([r[0, h] for r in (k0_ref, k1_ref, k2_ref, kc_ref)], axis=0)
        st = _dot(keys, qt_ref[0, h * HEAD_DIM:(h + 1) * HEAD_DIM, :])
        st_ref[h % 2, 0:n_win] = st[0:n_win] + tab_ref[0, h]
        st_ref[h % 2, n_win:] = st[n_win:]

    def attend(h):
        vals_t = jnp.concatenate([r[0, h] for r in (v0_ref, v1_ref, v2_ref, vc_ref)], axis=1)
        return _softmax_pv_t(st_ref[h % 2], vals_t)

    heads_t = []
    scores_to(0)
    for h in range(C_HEADS):
        if h + 1 < C_HEADS:
            scores_to(h + 1)
        heads_t.append(attend(h))
    o_ref[0] = jnp.concatenate(heads_t, axis=0).T.astype(BF16)


def _attention_c(tab, qt, k, vt, k_ctx, vt_ctx):
    bsz, _, n_tok = qt.shape
    tq = ROW_TILE
    nt = n_tok // tq
    slot = lambda s: (lambda b, i: jnp.clip(i - 1 + s, 0, nt - 1))
    k_spec = lambda s: pl.BlockSpec((1, C_HEADS, tq, HEAD_DIM), lambda b, i: (b, 0, slot(s)(b, i), 0))
    v_spec = lambda s: pl.BlockSpec((1, C_HEADS, 2 * HEAD_DIM, tq),
                                    lambda b, i: (b, 0, 0, slot(s)(b, i)))
    variant = lambda b, i: (jnp.where(i == 0, 0, jnp.where(i == nt - 1, 2, 1)), 0, 0, 0)
    return pl.pallas_call(
        _attn_c_kernel,
        grid=(bsz, nt),
        in_specs=[pl.BlockSpec((1, C_HEADS, 3 * tq, tq), variant),
                  pl.BlockSpec((1, Q_WIDTH, tq), lambda b, i: (b, 0, i))]
                 + [k_spec(s) for s in range(3)] + [v_spec(s) for s in range(3)]
                 + [pl.BlockSpec((1, C_HEADS, CTX_LEN, HEAD_DIM), lambda b, i: (b, 0, 0, 0)),
                    pl.BlockSpec((1, C_HEADS, 2 * HEAD_DIM, CTX_LEN), lambda b, i: (b, 0, 0, 0))],
        out_specs=pl.BlockSpec((1, tq, Q_WIDTH), lambda b, i: (b, i, 0)),
        out_shape=jax.ShapeDtypeStruct((bsz, n_tok, Q_WIDTH), BF16),
        scratch_shapes=[pltpu.VMEM((2, 3 * tq + CTX_LEN, tq), F32)],
        compiler_params=_params(("arbitrary", "arbitrary")),
        name="attn_nbr",
    )(tab, qt, k, k, k, vt, vt, vt, k_ctx, vt_ctx)


def _na_table_kernel(rowmat_ref, o_ref):
    rows = SEQ // GRID_W
    rpt = ROW_TILE // GRID_W
    nt = rows // rpt
    kh = min(NA_KH_MAX, rows)
    masked = jnp.full((GRID_W, GRID_W), NEG_INF, F32)
    for v, t in enumerate((0, 1, nt - 1)):
        for rl in range(rpt):
            r = rpt * t + rl
            rs = min(max(r - kh // 2, 0), rows - kh)
            for sk in range(3 * rpt):
                kr = rpt * (t - 1) + sk
                blk = rowmat_ref[0, kr - r + NA_KH_MAX - 1] if rs <= kr < rs + kh else masked
                o_ref[v, 0, sk * GRID_W:(sk + 1) * GRID_W, rl * GRID_W:(rl + 1) * GRID_W] = blk


def _neighbourhood_tables(rpb):
    c = np.arange(GRID_W)
    ws = np.clip(c - NA_KW // 2, 0, GRID_W - NA_KW)
    valid = (c[None, :] >= ws[:, None]) & (c[None, :] < ws[:, None] + NA_KW)
    coff = np.clip(c[None, :] - c[:, None] + NA_KW - 1, 0, 2 * NA_KW - 2)
    onehot = (coff[:, :, None] == np.arange(2 * NA_KW - 1)).astype(np.float32)
    rowmat = jnp.einsum("hij,ckj->hikc", rpb * LOG2_E, onehot, precision=lax.Precision.HIGHEST)
    rowmat = jnp.where(valid.T, rowmat, NEG_INF).astype(F32)
    n_off = 2 * NA_KH_MAX - 1
    return pl.pallas_call(
        _na_table_kernel,
        grid=(C_HEADS,),
        in_specs=[pl.BlockSpec((1, n_off, GRID_W, GRID_W), lambda h: (h, 0, 0, 0))],
        out_specs=pl.BlockSpec((3, 1, 3 * ROW_TILE, ROW_TILE), lambda h: (0, h, 0, 0)),
        out_shape=jax.ShapeDtypeStruct((3, C_HEADS, 3 * ROW_TILE, ROW_TILE), F32),
        compiler_params=_params(("arbitrary",)),
        name="na_tables",
    )(rowmat)


def _attn_ctx_kernel(sink_ref, qa_ref, ka_ref, va_ref, qb_ref, kb_ref, vb_ref,
                     qc_ref, kc_ref, vc_ref, oa_ref, ob_ref, oc_ref):
    group = A_Q_HEADS // A_KV_HEADS
    n_q = oa_ref.shape[1]
    outs = ([], [], [])
    for h in range(A_Q_HEADS):
        g = h // group
        hs = slice(h * HEAD_DIM, (h + 1) * HEAD_DIM)
        init = _flash_init_t(n_q)
        outs[0].append(_flash_finish_t(
            _flash_update_t(init, ka_ref[0, g], va_ref[0, g], qa_ref[0, hs, :]),
            sink=jnp.full((1, n_q), sink_ref[h], F32)))
        outs[1].append(_flash_finish_t(
            _flash_update_t(init, kb_ref[0, g], vb_ref[0, g], qb_ref[0, hs, :])))
        outs[2].append(_flash_finish_t(
            _flash_update_t(init, kc_ref[0, h], vc_ref[0, h], qc_ref[0, hs, :])))
    for o_ref, heads_t in zip((oa_ref, ob_ref, oc_ref), outs):
        o_ref[0] = jnp.concatenate(heads_t, axis=0).T.astype(BF16)


def _attention_ctx(sink, qa, ka, va, qb, kb, vb, qc, kc, vc):
    bsz = qa.shape[0]
    q_spec = pl.BlockSpec((1, Q_WIDTH, CTX_LEN), lambda b: (b, 0, 0))
    k_spec = lambda n: pl.BlockSpec((1, n, CTX_LEN, HEAD_DIM), lambda b: (b, 0, 0, 0))
    v_spec = lambda n: pl.BlockSpec((1, n, 2 * HEAD_DIM, CTX_LEN), lambda b: (b, 0, 0, 0))
    out = jax.ShapeDtypeStruct((bsz, CTX_LEN, Q_WIDTH), BF16)
    specs = [pl.BlockSpec(memory_space=pltpu.SMEM)]
    for n in (A_KV_HEADS, B_KV_HEADS, C_HEADS):
        specs += [q_spec, k_spec(n), v_spec(n)]
    return pl.pallas_call(
        _attn_ctx_kernel,
        grid=(bsz,),
        in_specs=specs,
        out_specs=[pl.BlockSpec((1, CTX_LEN, Q_WIDTH), lambda b: (b, 0, 0))] * 3,
        out_shape=[out] * 3,
        compiler_params=_params(("arbitrary",)),
        name="attn_ctx",
    )(sink, qa, ka, va, qb, kb, vb, qc, kc, vc)


def _post_kernel(x_ref, oa_ref, ob_ref, oc_ref, sg_ref, mod_ref, norms_ref,
                 wa_ref, wb_ref, wc_ref, wo_ref, w1_ref, w2_ref, o_ref):
    x = x_ref[0]
    merged = None
    for j, (o_r, w_r) in enumerate(((oa_ref, wa_ref), (ob_ref, wb_ref), (oc_ref, wc_ref))):
        gate = sg_ref[0, :, j * D_MODEL:(j + 1) * D_MODEL].astype(F32)
        term = gate * _dot(o_r[0], w_r[...])
        merged = term if merged is None else merged + term
    y = _dot(merged.astype(BF16), wo_ref[...])
    x1 = x + mod_ref[0, 2:3, :] * (_rms(y) * norms_ref[0:1, :])
    h2 = _rms(x1) * norms_ref[1:2, :]
    h2 = (h2 * (1.0 + mod_ref[0, 4:5, :]) + mod_ref[0, 3:4, :]).astype(BF16)
    z = None
    n_chunk = MLP_HIDDEN // D_MODEL
    for j in range(n_chunk):
        cs = slice(j * D_MODEL, (j + 1) * D_MODEL)
        a = jnp.maximum(_dot(h2, w1_ref[:, cs]), 0.0)
        zj = _dot((a * a).astype(BF16), w2_ref[cs, :])
        z = zj if z is None else z + zj
    o_ref[0] = x1 + mod_ref[0, 5:6, :] * (_rms(z) * norms_ref[2:3, :])


def _post_block(x, o_a, o_b, o_c, sg, mod, norms, wa, wb, wc, wo, w1, w2):
    bsz, n_tok, _ = x.shape
    tm = min(DENSE_ROW_TILE, n_tok)
    row = lambda w: pl.BlockSpec((1, tm, w), lambda b, i: (b, i, 0))
    return pl.pallas_call(
        _post_kernel,
        grid=(bsz, n_tok // tm),
        in_specs=[row(D_MODEL), row(Q_WIDTH), row(Q_WIDTH), row(Q_WIDTH), row(3 * D_MODEL),
                  pl.BlockSpec((1, 6, D_MODEL), lambda b, i: (b, 0, 0)),
                  _const_spec((8, D_MODEL)),
                  _const_spec((Q_WIDTH, D_MODEL)), _const_spec((Q_WIDTH, D_MODEL)),
                  _const_spec((Q_WIDTH, D_MODEL)), _const_spec((D_MODEL, D_MODEL)),
                  _const_spec((D_MODEL, MLP_HIDDEN)), _const_spec((MLP_HIDDEN, D_MODEL))],
        out_specs=row(D_MODEL),
        out_shape=jax.ShapeDtypeStruct((bsz, n_tok, D_MODEL), F32),
        compiler_params=_params(("arbitrary", "arbitrary")),
        name="post_mlp",
    )(x, o_a, o_b, o_c, sg, mod, norms, wa, wb, wc, wo, w1, w2)


def _rope_tables(n_tok):
    pos = jnp.arange(n_tok)
    row = (pos // GRID_W).astype(F32)
    col = (pos % GRID_W).astype(F32)
    n_freq = HEAD_DIM // 4
    freqs = ROPE_THETA ** (-jnp.arange(n_freq, dtype=F32) / n_freq)
    ang = jnp.concatenate([row[:, None] * freqs, col[:, None] * freqs], axis=-1)
    cos = jnp.repeat(jnp.cos(ang), 2, axis=-1)
    sign = jnp.tile(jnp.array([-1.0, 1.0], F32), HEAD_DIM // 2)
    sin = jnp.repeat(jnp.sin(ang), 2, axis=-1) * sign
    return jnp.tile(cos, (1, LANES // HEAD_DIM)), jnp.tile(sin, (1, LANES // HEAD_DIM))


def _extended_w_in(w_in_l):
    sizes = (Q_WIDTH, KV_WIDTH, KV_WIDTH, Q_WIDTH, KV_WIDTH, KV_WIDTH, Q_WIDTH, Q_WIDTH, Q_WIDTH)
    parts, start = [], 0
    for n in sizes:
        parts.append(w_in_l[:, start:start + n])
        start += n
    qa, ka, va, qb, kb, vb, qc, kc, vc = parts
    gates = w_in_l[:, start:]
    cols = [qa * Q_SCALE, ka, va, qb, kb, vb, qc * Q_SCALE, kc, vc, gates]
    return jnp.concatenate(cols, axis=1).astype(BF16)


def kernel(x, c, ctx, c_ctx, w_ada, b_ada, norm_mix_pre, norm_mix_post, w_in, sink_a, qnorm_b,
           knorm_b, rpb_c, w_br_a, w_br_b, w_br_c, w_out, norm_mlp_pre, norm_mlp_post,
           w_mlp_in, w_mlp_out):
    bsz, seq, _ = x.shape
    n_ctx = ctx.shape[1]
    cvec = jnp.concatenate([c, c_ctx[None, :], jnp.zeros((8 - bsz - 1, D_MODEL), F32)], axis=0)
    mod_all = _ada_modulation(cvec, w_ada, b_ada)

    cos_lat, sin_lat = _rope_tables(seq)
    cos_ctx = jnp.ones((n_ctx, LANES), F32)
    sin_ctx = jnp.zeros((n_ctx, LANES), F32)
    head_id = jnp.arange(QK_W) // HEAD_DIM
    gmat = (head_id[:, None] == head_id[None, :]).astype(BF16)
    tile2 = lambda g: jnp.tile(g, LANES // HEAD_DIM)
    swap1 = lambda g: g[jnp.arange(HEAD_DIM) ^ 1]

    x_lat, x_ctx = x, ctx
    for l in range(DEPTH):
        last = l == DEPTH - 1
        mod_lat = mod_all[l, :bsz].reshape(bsz, 6, D_MODEL)
        mod_ctx = jnp.broadcast_to(mod_all[l, bsz].reshape(1, 6, D_MODEL), (bsz, 6, D_MODEL))
        gq, gk = qnorm_b[l], knorm_b[l]
        vecs = jnp.stack([tile2(gq) * Q_SCALE, tile2(swap1(gq)) * Q_SCALE,
                          tile2(gk), tile2(swap1(gk))]
                         + [jnp.zeros((LANES,), F32)] * 4)
        score_bound = (1.02 * HEAD_DIM * Q_SCALE * jnp.max(jnp.abs(gq)) * jnp.max(jnp.abs(gk))
                       ).reshape(1).astype(F32)
        gpre = norm_mix_pre[l][None, :]
        w_ext = _extended_w_in(w_in[l])
        norms = jnp.stack([norm_mix_post[l], norm_mlp_pre[l], norm_mlp_post[l]]
                          + [jnp.zeros((D_MODEL,), F32)] * 5)
        wa, wb, wc = (w.astype(BF16) for w in (w_br_a[l], w_br_b[l], w_br_c[l]))
        wo = w_out[l].astype(BF16)
        w1 = w_mlp_in[l].astype(BF16)
        w2 = w_mlp_out[l].astype(BF16)
        tabs = _neighbourhood_tables(rpb_c[l])
        sink = sink_a[l] * LOG2_E

        proj_lat = _input_projection(x_lat, mod_lat, gpre, vecs, cos_lat, sin_lat, gmat, w_ext)
        proj_ctx = _input_projection(x_ctx, mod_ctx, gpre, vecs, cos_ctx, sin_ctx, gmat, w_ext)
        qa, ka, va, qb, kb, vb, qc, kc, vc, sg = proj_lat
        qa_c, ka_c, va_c, qb_c, kb_c, vb_c, qc_c, kc_c, vc_c, sg_c = proj_ctx

        o_a = _attention_a(sink, qa, ka, va, ka_c, va_c)
        o_b = _attention_b(score_bound, qb, kb, vb, kb_c, vb_c)
        o_c = _attention_c(tabs, qc, kc, vc, kc_c, vc_c)
        x_lat = _post_block(x_lat, o_a, o_b, o_c, sg, mod_lat, norms, wa, wb, wc, wo, w1, w2)
        if not last:
            o_a_c, o_b_c, o_c_c = _attention_ctx(sink, qa_c, ka_c, va_c, qb_c, kb_c, vb_c,
                                                 qc_c, kc_c, vc_c)
            x_ctx = _post_block(x_ctx, o_a_c, o_b_c, o_c_c, sg_c, mod_ctx, norms,
                                wa, wb, wc, wo, w1, w2)
    return x_lat
```

```python
import jax
import jax.numpy as jnp
import numpy as np
from jax import lax
from jax.experimental import pallas as pl
from jax.experimental.pallas import tpu as pltpu

D_MODEL = 1024
SEQ = 8192
DEPTH = 2
GRID_W = 64
CTX_LEN = 256
HEAD_DIM = 64
A_Q_HEADS = 6
A_KV_HEADS = 2
WINDOW = 128
B_Q_HEADS = 6
B_KV_HEADS = 2
C_HEADS = 6
NA_KH_MAX = 8
NA_KW = 16
Q_WIDTH = 6 * HEAD_DIM
KV_WIDTH = 2 * HEAD_DIM
MLP_HIDDEN = 4 * D_MODEL
ROPE_THETA = 10000.0
NORM_EPS = 1e-6
NEG_INF = -1e30
SM_SCALE = HEAD_DIM ** -0.5
LOG2_E = 1.4426950408889634
Q_SCALE = SM_SCALE * LOG2_E

V7X_VMEM_LIMIT_BYTES = 56 * 1024 * 1024
LANES = 128
ROW_TILE = 256
DENSE_ROW_TILE = 512
WIN_BLOCK = 128
KV_CHUNK = 512
GLOBAL_Q_TILE = 512
ADA_COLS = 1536
MAX_SAFE_SCORE_BOUND = 60.0
MAX_SAFE_SCORE_SPREAD = 2.0 * MAX_SAFE_SCORE_BOUND

BF16 = jnp.bfloat16
F32 = jnp.float32

A_OFF = 0
B_OFF = A_OFF + 640
C_OFF = B_OFF + 640
G_OFF = C_OFF + 1152
EXT_COLS = G_OFF + 3 * D_MODEL
QK_W = Q_WIDTH + KV_WIDTH


def _params(semantics):
    return pltpu.CompilerParams(dimension_semantics=semantics,
                                vmem_limit_bytes=V7X_VMEM_LIMIT_BYTES)


def _const_spec(shape):
    nd = len(shape)
    return pl.BlockSpec(shape, lambda *_: (0,) * nd, pipeline_mode=pl.Buffered(1))


def _dot(a, b):
    return jnp.dot(a, b, preferred_element_type=F32)


def _rms(x):
    return x * lax.rsqrt(jnp.mean(x * x, axis=-1, keepdims=True) + NORM_EPS)


def _ada_kernel(c_ref, w_ref, b_ref, o_ref):
    c = c_ref[...]
    a = c / (1.0 + jnp.exp(-c))
    w = w_ref[0]
    a_hi = a.astype(BF16)
    a_lo = (a - a_hi.astype(F32)).astype(BF16)
    w_hi = w.astype(BF16)
    w_lo = (w - w_hi.astype(F32)).astype(BF16)
    acc = _dot(a_hi, w_hi) + _dot(a_hi, w_lo) + _dot(a_lo, w_hi)
    o_ref[0] = acc + b_ref[0]


def _ada_modulation(cvec, w_ada, b_ada):
    n_col = 6 * D_MODEL
    return pl.pallas_call(
        _ada_kernel,
        grid=(DEPTH, n_col // ADA_COLS),
        in_specs=[
            pl.BlockSpec((8, D_MODEL), lambda l, j: (0, 0)),
            pl.BlockSpec((1, D_MODEL, ADA_COLS), lambda l, j: (l, 0, j)),
            pl.BlockSpec((1, 1, ADA_COLS), lambda l, j: (l, 0, j)),
        ],
        out_specs=pl.BlockSpec((1, 8, ADA_COLS), lambda l, j: (l, 0, j)),
        out_shape=jax.ShapeDtypeStruct((DEPTH, 8, n_col), F32),
        compiler_params=_params(("arbitrary", "arbitrary")),
        name="ada_mod",
    )(cvec, w_ada, b_ada.reshape(DEPTH, 1, n_col))


def _store_heads(k_ref, k, n_heads):
    for h in range(n_heads):
        k_ref[0, h] = k[:, h * HEAD_DIM:(h + 1) * HEAD_DIM].astype(BF16)


def _store_key_norms(n_ref, k, n_heads):
    rows = []
    for h in range(8):
        if h < n_heads:
            kh = k[:, h * HEAD_DIM:(h + 1) * HEAD_DIM]
            n2 = jnp.max(jnp.sum(kh * kh, axis=1, keepdims=True), axis=0, keepdims=True)
            rows.append(jnp.broadcast_to(n2, (1, LANES)))
        else:
            rows.append(jnp.zeros((1, LANES), F32))
    n_ref[0, 0] = jnp.concatenate(rows, axis=0)


def _store_values_t(vt_ref, v, n_heads):
    vt = v.T.astype(BF16)
    pad_row = lax.broadcasted_iota(jnp.int32, (HEAD_DIM, v.shape[0]), 0)
    ones_pad = jnp.where(pad_row == 0, 1.0, 0.0).astype(BF16)
    for h in range(n_heads):
        vt_ref[0, h, 0:HEAD_DIM, :] = vt[h * HEAD_DIM:(h + 1) * HEAD_DIM]
        vt_ref[0, h, HEAD_DIM:2 * HEAD_DIM, :] = ones_pad


def _pair_swap(x):
    lane = lax.broadcasted_iota(jnp.int32, x.shape, 1)
    return jnp.where((lane & 1) == 0, pltpu.roll(x, LANES - 1, axis=1), pltpu.roll(x, 1, axis=1))


def _proj_kernel(x_ref, mod_ref, gpre_ref, vecs_ref, cos_ref, sin_ref, gmat_ref, w_ref,
                 qa_ref, ka_ref, va_ref, qb_ref, kb_ref, vb_ref, qc_ref, kc_ref, vc_ref, sg_ref,
                 na_ref, nc_ref):
    x = x_ref[0]
    h = _rms(x) * gpre_ref[...]
    h = h * (1.0 + mod_ref[0, 1:2, :]) + mod_ref[0, 0:1, :]
    hb = h.astype(BF16)
    cos = cos_ref[...]
    sin = sin_ref[...]

    ra = _dot(hb, w_ref[:, A_OFF:A_OFF + QK_W + KV_WIDTH])
    for j in range(4):
        lo = j * LANES
        xa = ra[:, lo:lo + LANES]
        roped = xa * cos + _pair_swap(xa) * sin
        if j < 3:
            qa_ref[0, lo:lo + LANES, :] = roped.T.astype(BF16)
        else:
            _store_heads(ka_ref, roped, A_KV_HEADS)
            _store_key_norms(na_ref, roped, A_KV_HEADS)
    _store_values_t(va_ref, ra[:, QK_W:QK_W + KV_WIDTH], A_KV_HEADS)

    rb = _dot(hb, w_ref[:, B_OFF:B_OFF + QK_W + KV_WIDTH])
    xb = rb[:, 0:QK_W]
    sq = xb * xb
    sq_hi = sq.astype(BF16)
    sq_lo = (sq - sq_hi.astype(F32)).astype(BF16)
    head_ms = (_dot(sq_hi, gmat_ref[...]) + _dot(sq_lo, gmat_ref[...])) * (1.0 / HEAD_DIM)
    rinv = lax.rsqrt(head_ms + NORM_EPS)
    cq, sq_t = cos * vecs_ref[0:1, :], sin * vecs_ref[1:2, :]
    ck, sk_t = cos * vecs_ref[2:3, :], sin * vecs_ref[3:4, :]
    for j in range(4):
        lo = j * LANES
        c_t, s_t = (cq, sq_t) if j < 3 else (ck, sk_t)
        xj = rb[:, lo:lo + LANES]
        roped = rinv[:, lo:lo + LANES] * (xj * c_t + _pair_swap(xj) * s_t)
        if j < 3:
            qb_ref[0, lo:lo + LANES, :] = roped.T.astype(BF16)
        else:
            _store_heads(kb_ref, roped, B_KV_HEADS)
    _store_values_t(vb_ref, rb[:, QK_W:QK_W + KV_WIDTH], B_KV_HEADS)

    rc = _dot(hb, w_ref[:, C_OFF:C_OFF + 1152])
    qc_ref[0] = rc[:, 0:Q_WIDTH].T.astype(BF16)
    _store_heads(kc_ref, rc[:, Q_WIDTH:2 * Q_WIDTH], C_HEADS)
    _store_key_norms(nc_ref, rc[:, Q_WIDTH:2 * Q_WIDTH], C_HEADS)
    _store_values_t(vc_ref, rc[:, 2 * Q_WIDTH:3 * Q_WIDTH], C_HEADS)

    for j in range(3):
        lo = G_OFF + j * D_MODEL
        g = _dot(hb, w_ref[:, lo:lo + D_MODEL])
        sg_ref[0, :, j * D_MODEL:(j + 1) * D_MODEL] = (1.0 / (1.0 + jnp.exp(-g))).astype(BF16)


def _input_projection(x, mod, gpre, vecs, cos_t, sin_t, gmat, w_ext):
    bsz, n_tok, _ = x.shape
    tm = ROW_TILE
    bf = lambda *s: jax.ShapeDtypeStruct(s, BF16)
    q_spec = pl.BlockSpec((1, Q_WIDTH, tm), lambda b, i: (b, 0, i))
    k_spec = lambda n: pl.BlockSpec((1, n, tm, HEAD_DIM), lambda b, i: (b, 0, i, 0))
    v_spec = lambda n: pl.BlockSpec((1, n, 2 * HEAD_DIM, tm), lambda b, i: (b, 0, 0, i))
    qkv_specs, qkv_shapes = [], []
    for n in (A_KV_HEADS, B_KV_HEADS, C_HEADS):
        qkv_specs += [q_spec, k_spec(n), v_spec(n)]
        qkv_shapes += [bf(bsz, Q_WIDTH, n_tok), bf(bsz, n, n_tok, HEAD_DIM),
                       bf(bsz, n, 2 * HEAD_DIM, n_tok)]
    return pl.pallas_call(
        _proj_kernel,
        grid=(bsz, n_tok // tm),
        in_specs=[
            pl.BlockSpec((1, tm, D_MODEL), lambda b, i: (b, i, 0)),
            pl.BlockSpec((1, 6, D_MODEL), lambda b, i: (b, 0, 0)),
            _const_spec((1, D_MODEL)),
            _const_spec((8, LANES)),
            pl.BlockSpec((tm, LANES), lambda b, i: (i, 0)),
            pl.BlockSpec((tm, LANES), lambda b, i: (i, 0)),
            _const_spec((QK_W, QK_W)),
            _const_spec((D_MODEL, EXT_COLS)),
        ],
        out_specs=qkv_specs + [pl.BlockSpec((1, tm, 3 * D_MODEL), lambda b, i: (b, i, 0))]
                  + [pl.BlockSpec((1, 1, 8, LANES), lambda b, i: (b, i, 0, 0))] * 2,
        out_shape=qkv_shapes + [bf(bsz, n_tok, 3 * D_MODEL)]
                  + [jax.ShapeDtypeStruct((bsz, n_tok // tm, 8, LANES), F32)] * 2,
        compiler_params=_params(("arbitrary", "arbitrary")),
        name="in_proj",
    )(x, mod, gpre, vecs, cos_t, sin_t, gmat, w_ext)


def _flash_update_t(carry, k, vt, qt, bias=None):
    m, acc = carry
    st = _dot(k, qt)
    if bias is not None:
        st = st + bias
    m_new = jnp.maximum(m, jnp.max(st, axis=0, keepdims=True))
    alpha = jnp.exp2(m - m_new)
    p = jnp.exp2(st - m_new).astype(BF16)
    return m_new, alpha * acc + _dot(vt, p)


def _bounded_update_t(acc, k, vt, qt, bound):
    p = jnp.exp2(_dot(k, qt) - bound).astype(BF16)
    return acc + _dot(vt, p)


def _flash_init_t(n_q):
    return jnp.full((1, n_q), NEG_INF, F32), jnp.zeros((2 * HEAD_DIM, n_q), F32)


def _flash_finish_t(carry, sink=None):
    m, acc = carry
    denom = acc[HEAD_DIM:HEAD_DIM + 1]
    if sink is not None:
        denom = denom + jnp.exp2(sink - m)
    return acc[0:HEAD_DIM] / denom


def _score_bound_t(kn2, qt):
    qf = qt.astype(F32)
    return 1.02 * jnp.sqrt(kn2 * jnp.sum(qf * qf, axis=0, keepdims=True))


def _softmax_pv_t(st, vt, sink=None, ref=None):
    m = jnp.max(st, axis=0, keepdims=True) if ref is None else ref
    acc = _dot(vt, jnp.exp2(st - m).astype(BF16))
    denom = acc[HEAD_DIM:HEAD_DIM + 1]
    if sink is not None:
        denom = denom + jnp.exp2(sink - m)
    return acc[0:HEAD_DIM] / denom


def _stack_heads_t(qt_ref, g, group):
    return jnp.concatenate(
        [qt_ref[0, (g * group + j) * HEAD_DIM:(g * group + j + 1) * HEAD_DIM, :]
         for j in range(group)], axis=1)


def _sink_row(sink_ref, g, group, n_q):
    return jnp.concatenate(
        [jnp.full((1, n_q), sink_ref[g * group + j], F32) for j in range(group)], axis=1)


def _attn_a_kernel(sink_ref, kn_ref, qt_ref, k0_ref, k1_ref, k2_ref, k3_ref,
                   v0_ref, v1_ref, v2_ref, v3_ref, kc_ref, vc_ref, o_ref, st_ref):
    t = pl.program_id(1)
    nt = pl.num_programs(1)
    tq = o_ref.shape[1]
    n_win = 4 * WIN_BLOCK
    group = A_Q_HEADS // A_KV_HEADS
    kj = lax.broadcasted_iota(jnp.int32, (n_win, tq), 0)
    qi = lax.broadcasted_iota(jnp.int32, (n_win, tq), 1)
    rel = kj - WIN_BLOCK - qi
    valid = (rel <= WINDOW) & (rel >= -WINDOW)
    valid = valid & ((kj >= WIN_BLOCK) | (t > 0)) & ((kj < 3 * WIN_BLOCK) | (t < nt - 1))
    mask = jnp.where(valid, 0.0, NEG_INF).astype(F32)
    mask = jnp.concatenate([mask] * group, axis=1)

    def scores_to(g):
        keys = jnp.concatenate([r[0, g] for r in (k0_ref, k1_ref, k2_ref, k3_ref, kc_ref)], axis=0)
        st = _dot(keys, _stack_heads_t(qt_ref, g, group))
        st_ref[g, 0:n_win] = st[0:n_win] + mask
        st_ref[g, n_win:] = st[n_win:]

    def attend(g, ref):
        vals_t = jnp.concatenate([r[0, g] for r in (v0_ref, v1_ref, v2_ref, v3_ref, vc_ref)], axis=1)
        og = _softmax_pv_t(st_ref[g], vals_t, sink=_sink_row(sink_ref, g, group, tq), ref=ref)
        return [og[:, j * tq:(j + 1) * tq] for j in range(group)]

    def run(refs):
        heads_t = []
        scores_to(0)
        for g in range(A_KV_HEADS):
            if g + 1 < A_KV_HEADS:
                scores_to(g + 1)
            heads_t += attend(g, refs[g])
        o_ref[0] = jnp.concatenate(heads_t, axis=0).T.astype(BF16)

    bounds = [_score_bound_t(kn_ref[0, 0, g:g + 1, 0:1], _stack_heads_t(qt_ref, g, group))
              for g in range(A_KV_HEADS)]
    worst = bounds[0]
    for b in bounds[1:]:
        worst = jnp.maximum(worst, b)
    bounded = 2.0 * jnp.max(worst) <= MAX_SAFE_SCORE_SPREAD

    @pl.when(bounded)
    def _():
        run(bounds)

    @pl.when(jnp.logical_not(bounded))
    def _():
        run([None] * A_KV_HEADS)


def _attention_a(sink, kn, qt, k, vt, k_ctx, vt_ctx):
    bsz, _, n_tok = qt.shape
    tq = ROW_TILE
    nb = n_tok // WIN_BLOCK
    per = tq // WIN_BLOCK
    slot = lambda s: (lambda b, i: jnp.clip(per * i - 1 + s, 0, nb - 1))
    k_spec = lambda s: pl.BlockSpec((1, A_KV_HEADS, WIN_BLOCK, HEAD_DIM),
                                    lambda b, i: (b, 0, slot(s)(b, i), 0))
    v_spec = lambda s: pl.BlockSpec((1, A_KV_HEADS, 2 * HEAD_DIM, WIN_BLOCK),
                                    lambda b, i: (b, 0, 0, slot(s)(b, i)))
    return pl.pallas_call(
        _attn_a_kernel,
        grid=(bsz, n_tok // tq),
        in_specs=[pl.BlockSpec(memory_space=pltpu.SMEM),
                  pl.BlockSpec((1, 1, 8, LANES), lambda b, i: (b, i, 0, 0)),
                  pl.BlockSpec((1, Q_WIDTH, tq), lambda b, i: (b, 0, i))]
                 + [k_spec(s) for s in range(4)] + [v_spec(s) for s in range(4)]
                 + [pl.BlockSpec((1, A_KV_HEADS, CTX_LEN, HEAD_DIM), lambda b, i: (b, 0, 0, 0)),
                    pl.BlockSpec((1, A_KV_HEADS, 2 * HEAD_DIM, CTX_LEN), lambda b, i: (b, 0, 0, 0))],
        out_specs=pl.BlockSpec((1, tq, Q_WIDTH), lambda b, i: (b, i, 0)),
        out_shape=jax.ShapeDtypeStruct((bsz, n_tok, Q_WIDTH), BF16),
        scratch_shapes=[pltpu.VMEM((A_KV_HEADS, 4 * WIN_BLOCK + CTX_LEN,
                                    (A_Q_HEADS // A_KV_HEADS) * tq), F32)],
        compiler_params=_params(("arbitrary", "arbitrary")),
        name="attn_window",
    )(sink, kn, qt, k, k, k, k, vt, vt, vt, vt, k_ctx, vt_ctx)


def _attn_b_kernel(bound_ref, qt_ref, kc_ref, vtc_ref, k_ref, vt_ref, o_ref, st0_ref, st1_ref):
    st_ref = (st0_ref, st1_ref)
    tq = o_ref.shape[1]
    n_chunks = k_ref.shape[2] // KV_CHUNK
    group = B_Q_HEADS // B_KV_HEADS
    groups = range(B_KV_HEADS)
    qts = [_stack_heads_t(qt_ref, g, group) for g in groups]

    def chunk(c, g):
        start = pl.multiple_of(c * KV_CHUNK, KV_CHUNK)
        return k_ref[0, g, pl.ds(start, KV_CHUNK), :], vt_ref[0, g, :, pl.ds(start, KV_CHUNK)]

    def write(outs_t):
        heads_t = [og[:, j * tq:(j + 1) * tq] for og in outs_t for j in range(group)]
        o_ref[0] = jnp.concatenate(heads_t, axis=0).T.astype(BF16)

    bound = bound_ref[0]
    bounded = bound <= MAX_SAFE_SCORE_BOUND

    @pl.when(bounded)
    def _():
        def scores_to(slot, c):
            for g in groups:
                st_ref[slot][g] = _dot(chunk(c, g)[0], qts[g])

        def accumulate(accs, slot, c):
            return tuple(accs[g] + _dot(chunk(c, g)[1],
                                        jnp.exp2(st_ref[slot][g] - bound).astype(BF16))
                         for g in groups)

        def stage(accs, cur, c):
            out = []
            for g in groups:
                k_next = chunk(c + 1, g)[0]
                vt_cur = chunk(c, g)[1]
                parts = []
                for j in range(group):
                    ls = slice(j * tq, (j + 1) * tq)
                    st_ref[1 - cur][g, :, ls] = _dot(k_next, qts[g][:, ls])
                    p = jnp.exp2(st_ref[cur][g, :, ls] - bound).astype(BF16)
                    parts.append(accs[g][:, ls] + _dot(vt_cur, p))
                out.append(jnp.concatenate(parts, axis=1))
            return tuple(out)

        def body(i, accs):
            c = 2 * i
            return stage(stage(accs, 0, c), 1, c + 1)

        scores_to(0, 0)
        accs = tuple(_bounded_update_t(jnp.zeros((2 * HEAD_DIM, group * tq), F32),
                                       kc_ref[0, g], vtc_ref[0, g], qts[g], bound) for g in groups)
        accs = lax.fori_loop(0, n_chunks // 2 - 1, body, accs)
        accs = stage(accs, 0, n_chunks - 2)
        accs = accumulate(accs, 1, n_chunks - 1)
        write([acc[0:HEAD_DIM] / acc[HEAD_DIM:HEAD_DIM + 1] for acc in accs])

    @pl.when(jnp.logical_not(bounded))
    def _():
        carries = tuple(_flash_update_t(_flash_init_t(group * tq), kc_ref[0, g], vtc_ref[0, g], qts[g])
                        for g in groups)
        carries = lax.fori_loop(
            0, n_chunks,
            lambda c, carries: tuple(_flash_update_t(carries[g], *chunk(c, g), qts[g])
                                     for g in groups), carries)
        write([_flash_finish_t(carry) for carry in carries])


def _attention_b(bound, qt, k, vt, k_ctx, vt_ctx):
    bsz, _, n_tok = qt.shape
    tq = GLOBAL_Q_TILE
    keys = lambda n: pl.BlockSpec((1, B_KV_HEADS, n, HEAD_DIM), lambda b, i: (b, 0, 0, 0))
    vals = lambda n: pl.BlockSpec((1, B_KV_HEADS, 2 * HEAD_DIM, n), lambda b, i: (b, 0, 0, 0))
    return pl.pallas_call(
        _attn_b_kernel,
        grid=(bsz, n_tok // tq),
        in_specs=[pl.BlockSpec(memory_space=pltpu.SMEM),
                  pl.BlockSpec((1, Q_WIDTH, tq), lambda b, i: (b, 0, i)),
                  keys(CTX_LEN), vals(CTX_LEN), keys(n_tok), vals(n_tok)],
        out_specs=pl.BlockSpec((1, tq, Q_WIDTH), lambda b, i: (b, i, 0)),
        out_shape=jax.ShapeDtypeStruct((bsz, n_tok, Q_WIDTH), BF16),
        scratch_shapes=[pltpu.VMEM((B_KV_HEADS, KV_CHUNK, (B_Q_HEADS // B_KV_HEADS) * tq), F32)] * 2,
        compiler_params=_params(("arbitrary", "arbitrary")),
        name="attn_global",
    )(bound, qt, k_ctx, vt_ctx, k, vt)


def _attn_c_kernel(brange_ref, kn_ref, tab_ref, qt_ref, k0_ref, k1_ref, k2_ref,
                   v0_ref, v1_ref, v2_ref, kc_ref, vc_ref, o_ref, st_ref):
    n_win = tab_ref.shape[2]

    def scores_to(h):
        keys = jnp.concatenate([r[0, h] for r in (k0_ref, k1_ref, k2_ref, kc_ref)], axis=0)
        st = _dot(keys, qt_ref[0, h * HEAD_DIM:(h + 1) * HEAD_DIM, :])
        st_ref[h % 2, 0:n_win] = st[0:n_win] + tab_ref[0, h]
        st_ref[h % 2, n_win:] = st[n_win:]

    def attend(h, ref):
        vals_t = jnp.concatenate([r[0, h] for r in (v0_ref, v1_ref, v2_ref, vc_ref)], axis=1)
        return _softmax_pv_t(st_ref[h % 2], vals_t, ref=ref)

    def run(refs):
        heads_t = []
        scores_to(0)
        for h in range(C_HEADS):
            if h + 1 < C_HEADS:
                scores_to(h + 1)
            heads_t.append(attend(h, refs[h]))
        o_ref[0] = jnp.concatenate(heads_t, axis=0).T.astype(BF16)

    bias_hi, bias_lo = brange_ref[0], brange_ref[1]
    bounds = [_score_bound_t(kn_ref[0, 0, h:h + 1, 0:1], qt_ref[0, h * HEAD_DIM:(h + 1) * HEAD_DIM, :])
              for h in range(C_HEADS)]
    worst = bounds[0]
    for b in bounds[1:]:
        worst = jnp.maximum(worst, b)
    bounded = 2.0 * jnp.max(worst) + (bias_hi - bias_lo) <= MAX_SAFE_SCORE_SPREAD

    @pl.when(bounded)
    def _():
        run([b + bias_hi for b in bounds])

    @pl.when(jnp.logical_not(bounded))
    def _():
        run([None] * C_HEADS)


def _attention_c(bias_range, kn, tab, qt, k, vt, k_ctx, vt_ctx):
    bsz, _, n_tok = qt.shape
    tq = ROW_TILE
    nt = n_tok // tq
    slot = lambda s: (lambda b, i: jnp.clip(i - 1 + s, 0, nt - 1))
    k_spec = lambda s: pl.BlockSpec((1, C_HEADS, tq, HEAD_DIM), lambda b, i: (b, 0, slot(s)(b, i), 0))
    v_spec = lambda s: pl.BlockSpec((1, C_HEADS, 2 * HEAD_DIM, tq),
                                    lambda b, i: (b, 0, 0, slot(s)(b, i)))
    variant = lambda b, i: (jnp.where(i == 0, 0, jnp.where(i == nt - 1, 2, 1)), 0, 0, 0)
    return pl.pallas_call(
        _attn_c_kernel,
        grid=(bsz, nt),
        in_specs=[pl.BlockSpec(memory_space=pltpu.SMEM),
                  pl.BlockSpec((1, 1, 8, LANES), lambda b, i: (b, i, 0, 0)),
                  pl.BlockSpec((1, C_HEADS, 3 * tq, tq), variant),
                  pl.BlockSpec((1, Q_WIDTH, tq), lambda b, i: (b, 0, i))]
                 + [k_spec(s) for s in range(3)] + [v_spec(s) for s in range(3)]
                 + [pl.BlockSpec((1, C_HEADS, CTX_LEN, HEAD_DIM), lambda b, i: (b, 0, 0, 0)),
                    pl.BlockSpec((1, C_HEADS, 2 * HEAD_DIM, CTX_LEN), lambda b, i: (b, 0, 0, 0))],
        out_specs=pl.BlockSpec((1, tq, Q_WIDTH), lambda b, i: (b, i, 0)),
        out_shape=jax.ShapeDtypeStruct((bsz, n_tok, Q_WIDTH), BF16),
        scratch_shapes=[pltpu.VMEM((2, 3 * tq + CTX_LEN, tq), F32)],
        compiler_params=_params(("arbitrary", "arbitrary")),
        name="attn_nbr",
    )(bias_range, kn, tab, qt, k, k, k, vt, vt, vt, k_ctx, vt_ctx)


def _na_table_kernel(rowmat_ref, o_ref):
    rows = SEQ // GRID_W
    rpt = ROW_TILE // GRID_W
    nt = rows // rpt
    kh = min(NA_KH_MAX, rows)
    masked = jnp.full((GRID_W, GRID_W), NEG_INF, F32)
    for v, t in enumerate((0, 1, nt - 1)):
        for rl in range(rpt):
            r = rpt * t + rl
            rs = min(max(r - kh // 2, 0), rows - kh)
            for sk in range(3 * rpt):
                kr = rpt * (t - 1) + sk
                blk = rowmat_ref[0, kr - r + NA_KH_MAX - 1] if rs <= kr < rs + kh else masked
                o_ref[v, 0, sk * GRID_W:(sk + 1) * GRID_W, rl * GRID_W:(rl + 1) * GRID_W] = blk


def _neighbourhood_tables(rpb):
    c = np.arange(GRID_W)
    ws = np.clip(c - NA_KW // 2, 0, GRID_W - NA_KW)
    valid = (c[None, :] >= ws[:, None]) & (c[None, :] < ws[:, None] + NA_KW)
    coff = np.clip(c[None, :] - c[:, None] + NA_KW - 1, 0, 2 * NA_KW - 2)
    onehot = (coff[:, :, None] == np.arange(2 * NA_KW - 1)).astype(np.float32)
    rowmat = jnp.einsum("hij,ckj->hikc", rpb * LOG2_E, onehot, precision=lax.Precision.HIGHEST)
    rowmat = jnp.where(valid.T, rowmat, NEG_INF).astype(F32)
    n_off = 2 * NA_KH_MAX - 1
    return pl.pallas_call(
        _na_table_kernel,
        grid=(C_HEADS,),
        in_specs=[pl.BlockSpec((1, n_off, GRID_W, GRID_W), lambda h: (h, 0, 0, 0))],
        out_specs=pl.BlockSpec((3, 1, 3 * ROW_TILE, ROW_TILE), lambda h: (0, h, 0, 0)),
        out_shape=jax.ShapeDtypeStruct((3, C_HEADS, 3 * ROW_TILE, ROW_TILE), F32),
        compiler_params=_params(("arbitrary",)),
        name="na_tables",
    )(rowmat)


def _attn_ctx_kernel(sink_ref, qa_ref, ka_ref, va_ref, qb_ref, kb_ref, vb_ref,
                     qc_ref, kc_ref, vc_ref, oa_ref, ob_ref, oc_ref):
    group = A_Q_HEADS // A_KV_HEADS
    n_q = oa_ref.shape[1]
    outs = ([], [], [])
    for h in range(A_Q_HEADS):
        g = h // group
        hs = slice(h * HEAD_DIM, (h + 1) * HEAD_DIM)
        init = _flash_init_t(n_q)
        outs[0].append(_flash_finish_t(
            _flash_update_t(init, ka_ref[0, g], va_ref[0, g], qa_ref[0, hs, :]),
            sink=jnp.full((1, n_q), sink_ref[h], F32)))
        outs[1].append(_flash_finish_t(
            _flash_update_t(init, kb_ref[0, g], vb_ref[0, g], qb_ref[0, hs, :])))
        outs[2].append(_flash_finish_t(
            _flash_update_t(init, kc_ref[0, h], vc_ref[0, h], qc_ref[0, hs, :])))
    for o_ref, heads_t in zip((oa_ref, ob_ref, oc_ref), outs):
        o_ref[0] = jnp.concatenate(heads_t, axis=0).T.astype(BF16)


def _attention_ctx(sink, qa, ka, va, qb, kb, vb, qc, kc, vc):
    bsz = qa.shape[0]
    q_spec = pl.BlockSpec((1, Q_WIDTH, CTX_LEN), lambda b: (b, 0, 0))
    k_spec = lambda n: pl.BlockSpec((1, n, CTX_LEN, HEAD_DIM), lambda b: (b, 0, 0, 0))
    v_spec = lambda n: pl.BlockSpec((1, n, 2 * HEAD_DIM, CTX_LEN), lambda b: (b, 0, 0, 0))
    out = jax.ShapeDtypeStruct((bsz, CTX_LEN, Q_WIDTH), BF16)
    specs = [pl.BlockSpec(memory_space=pltpu.SMEM)]
    for n in (A_KV_HEADS, B_KV_HEADS, C_HEADS):
        specs += [q_spec, k_spec(n), v_spec(n)]
    return pl.pallas_call(
        _attn_ctx_kernel,
        grid=(bsz,),
        in_specs=specs,
        out_specs=[pl.BlockSpec((1, CTX_LEN, Q_WIDTH), lambda b: (b, 0, 0))] * 3,
        out_shape=[out] * 3,
        compiler_params=_params(("arbitrary",)),
        name="attn_ctx",
    )(sink, qa, ka, va, qb, kb, vb, qc, kc, vc)


def _post_kernel(x_ref, oa_ref, ob_ref, oc_ref, sg_ref, mod_ref, norms_ref,
                 wa_ref, wb_ref, wc_ref, wo_ref, w1_ref, w2_ref, o_ref):
    x = x_ref[0]
    merged = None
    for j, (o_r, w_r) in enumerate(((oa_ref, wa_ref), (ob_ref, wb_ref), (oc_ref, wc_ref))):
        gate = sg_ref[0, :, j * D_MODEL:(j + 1) * D_MODEL].astype(F32)
        term = gate * _dot(o_r[0], w_r[...])
        merged = term if merged is None else merged + term
    y = _dot(merged.astype(BF16), wo_ref[...])
    x1 = x + mod_ref[0, 2:3, :] * (_rms(y) * norms_ref[0:1, :])
    h2 = _rms(x1) * norms_ref[1:2, :]
    h2 = (h2 * (1.0 + mod_ref[0, 4:5, :]) + mod_ref[0, 3:4, :]).astype(BF16)
    z = None
    n_chunk = MLP_HIDDEN // D_MODEL
    for j in range(n_chunk):
        cs = slice(j * D_MODEL, (j + 1) * D_MODEL)
        a = jnp.maximum(_dot(h2, w1_ref[:, cs]), 0.0)
        zj = _dot((a * a).astype(BF16), w2_ref[cs, :])
        z = zj if z is None else z + zj
    o_ref[0] = x1 + mod_ref[0, 5:6, :] * (_rms(z) * norms_ref[2:3, :])


def _post_block(x, o_a, o_b, o_c, sg, mod, norms, wa, wb, wc, wo, w1, w2):
    bsz, n_tok, _ = x.shape
    tm = min(DENSE_ROW_TILE, n_tok)
    row = lambda w: pl.BlockSpec((1, tm, w), lambda b, i: (b, i, 0))
    return pl.pallas_call(
        _post_kernel,
        grid=(bsz, n_tok // tm),
        in_specs=[row(D_MODEL), row(Q_WIDTH), row(Q_WIDTH), row(Q_WIDTH), row(3 * D_MODEL),
                  pl.BlockSpec((1, 6, D_MODEL), lambda b, i: (b, 0, 0)),
                  _const_spec((8, D_MODEL)),
                  _const_spec((Q_WIDTH, D_MODEL)), _const_spec((Q_WIDTH, D_MODEL)),
                  _const_spec((Q_WIDTH, D_MODEL)), _const_spec((D_MODEL, D_MODEL)),
                  _const_spec((D_MODEL, MLP_HIDDEN)), _const_spec((MLP_HIDDEN, D_MODEL))],
        out_specs=row(D_MODEL),
        out_shape=jax.ShapeDtypeStruct((bsz, n_tok, D_MODEL), F32),
        compiler_params=_params(("arbitrary", "arbitrary")),
        name="post_mlp",
    )(x, o_a, o_b, o_c, sg, mod, norms, wa, wb, wc, wo, w1, w2)


def _rope_tables(n_tok):
    pos = jnp.arange(n_tok)
    row = (pos // GRID_W).astype(F32)
    col = (pos % GRID_W).astype(F32)
    n_freq = HEAD_DIM // 4
    freqs = ROPE_THETA ** (-jnp.arange(n_freq, dtype=F32) / n_freq)
    ang = jnp.concatenate([row[:, None] * freqs, col[:, None] * freqs], axis=-1)
    cos = jnp.repeat(jnp.cos(ang), 2, axis=-1)
    sign = jnp.tile(jnp.array([-1.0, 1.0], F32), HEAD_DIM // 2)
    sin = jnp.repeat(jnp.sin(ang), 2, axis=-1) * sign
    return jnp.tile(cos, (1, LANES // HEAD_DIM)), jnp.tile(sin, (1, LANES // HEAD_DIM))


def _extended_w_in(w_in_l):
    sizes = (Q_WIDTH, KV_WIDTH, KV_WIDTH, Q_WIDTH, KV_WIDTH, KV_WIDTH, Q_WIDTH, Q_WIDTH, Q_WIDTH)
    parts, start = [], 0
    for n in sizes:
        parts.append(w_in_l[:, start:start + n])
        start += n
    qa, ka, va, qb, kb, vb, qc, kc, vc = parts
    gates = w_in_l[:, start:]
    cols = [qa * Q_SCALE, ka, va, qb, kb, vb, qc * Q_SCALE, kc, vc, gates]
    return jnp.concatenate(cols, axis=1).astype(BF16)


def _window_key_norms(kn, kn_ctx):
    prev = jnp.concatenate([kn[:, :1], kn[:, :-1]], axis=1)
    nxt = jnp.concatenate([kn[:, 1:], kn[:, -1:]], axis=1)
    return jnp.maximum(jnp.maximum(prev, kn), jnp.maximum(nxt, kn_ctx))


def kernel(x, c, ctx, c_ctx, w_ada, b_ada, norm_mix_pre, norm_mix_post, w_in, sink_a, qnorm_b,
           knorm_b, rpb_c, w_br_a, w_br_b, w_br_c, w_out, norm_mlp_pre, norm_mlp_post,
           w_mlp_in, w_mlp_out):
    bsz, seq, _ = x.shape
    n_ctx = ctx.shape[1]
    cvec = jnp.concatenate([c, c_ctx[None, :], jnp.zeros((8 - bsz - 1, D_MODEL), F32)], axis=0)
    mod_all = _ada_modulation(cvec, w_ada, b_ada)

    cos_lat, sin_lat = _rope_tables(seq)
    cos_ctx = jnp.ones((n_ctx, LANES), F32)
    sin_ctx = jnp.zeros((n_ctx, LANES), F32)
    head_id = jnp.arange(QK_W) // HEAD_DIM
    gmat = (head_id[:, None] == head_id[None, :]).astype(BF16)
    tile2 = lambda g: jnp.tile(g, LANES // HEAD_DIM)
    swap1 = lambda g: g[jnp.arange(HEAD_DIM) ^ 1]

    x_lat, x_ctx = x, ctx
    for l in range(DEPTH):
        last = l == DEPTH - 1
        mod_lat = mod_all[l, :bsz].reshape(bsz, 6, D_MODEL)
        mod_ctx = jnp.broadcast_to(mod_all[l, bsz].reshape(1, 6, D_MODEL), (bsz, 6, D_MODEL))
        gq, gk = qnorm_b[l], knorm_b[l]
        vecs = jnp.stack([tile2(gq) * Q_SCALE, tile2(swap1(gq)) * Q_SCALE,
                          tile2(gk), tile2(swap1(gk))]
                         + [jnp.zeros((LANES,), F32)] * 4)
        score_bound = (1.02 * HEAD_DIM * Q_SCALE * jnp.max(jnp.abs(gq)) * jnp.max(jnp.abs(gk))
                       ).reshape(1).astype(F32)
        gpre = norm_mix_pre[l][None, :]
        w_ext = _extended_w_in(w_in[l])
        norms = jnp.stack([norm_mix_post[l], norm_mlp_pre[l], norm_mlp_post[l]]
                          + [jnp.zeros((D_MODEL,), F32)] * 5)
        wa, wb, wc = (w.astype(BF16) for w in (w_br_a[l], w_br_b[l], w_br_c[l]))
        wo = w_out[l].astype(BF16)
        w1 = w_mlp_in[l].astype(BF16)
        w2 = w_mlp_out[l].astype(BF16)
        tabs = _neighbourhood_tables(rpb_c[l])
        sink = sink_a[l] * LOG2_E

        proj_lat = _input_projection(x_lat, mod_lat, gpre, vecs, cos_lat, sin_lat, gmat, w_ext)
        proj_ctx = _input_projection(x_ctx, mod_ctx, gpre, vecs, cos_ctx, sin_ctx, gmat, w_ext)
        qa, ka, va, qb, kb, vb, qc, kc, vc, sg, na, nc = proj_lat
        qa_c, ka_c, va_c, qb_c, kb_c, vb_c, qc_c, kc_c, vc_c, sg_c, na_c, nc_c = proj_ctx

        o_a = _attention_a(sink, _window_key_norms(na, na_c), qa, ka, va, ka_c, va_c)
        o_b = _attention_b(score_bound, qb, kb, vb, kb_c, vb_c)
        bias = rpb_c[l] * LOG2_E
        bias_range = jnp.stack([jnp.maximum(jnp.max(bias), 0.0), jnp.minimum(jnp.min(bias), 0.0)])
        o_c = _attention_c(bias_range, _window_key_norms(nc, nc_c), tabs, qc, kc, vc, kc_c, vc_c)
        x_lat = _post_block(x_lat, o_a, o_b, o_c, sg, mod_lat, norms, wa, wb, wc, wo, w1, w2)
        if not last:
            o_a_c, o_b_c, o_c_c = _attention_ctx(sink, qa_c, ka_c, va_c, qb_c, kb_c, vb_c,
                                                 qc_c, kc_c, vc_c)
            x_ctx = _post_block(x_ctx, o_a_c, o_b_c, o_c_c, sg_c, mod_ctx, norms,
                                wa, wb, wc, wo, w1, w2)
    return x_lat
```

```python
import jax
import jax.numpy as jnp
import numpy as np
from jax import lax
from jax.experimental import pallas as pl
from jax.experimental.pallas import tpu as pltpu

D_MODEL = 1024
SEQ = 8192
DEPTH = 2
GRID_W = 64
CTX_LEN = 256
HEAD_DIM = 64
A_Q_HEADS = 6
A_KV_HEADS = 2
WINDOW = 128
B_Q_HEADS = 6
B_KV_HEADS = 2
C_HEADS = 6
NA_KH_MAX = 8
NA_KW = 16
Q_WIDTH = 6 * HEAD_DIM
KV_WIDTH = 2 * HEAD_DIM
MLP_HIDDEN = 4 * D_MODEL
ROPE_THETA = 10000.0
NORM_EPS = 1e-6
NEG_INF = -1e30
SM_SCALE = HEAD_DIM ** -0.5
LOG2_E = 1.4426950408889634
Q_SCALE = SM_SCALE * LOG2_E

V7X_VMEM_LIMIT_BYTES = 56 * 1024 * 1024
LANES = 128
ROW_TILE = 256
DENSE_ROW_TILE = 512
WIN_BLOCK = 128
KV_CHUNK = 512
GLOBAL_Q_TILE = 512
ADA_COLS = 1536
MAX_SAFE_SCORE_BOUND = 60.0
MAX_SAFE_SCORE_SPREAD = 2.0 * MAX_SAFE_SCORE_BOUND

BF16 = jnp.bfloat16
F32 = jnp.float32

A_OFF = 0
B_OFF = A_OFF + 640
C_OFF = B_OFF + 640
G_OFF = C_OFF + 1152
EXT_COLS = G_OFF + 3 * D_MODEL
QK_W = Q_WIDTH + KV_WIDTH


def _params(semantics):
    return pltpu.CompilerParams(dimension_semantics=semantics,
                                vmem_limit_bytes=V7X_VMEM_LIMIT_BYTES)


def _const_spec(shape):
    nd = len(shape)
    return pl.BlockSpec(shape, lambda *_: (0,) * nd, pipeline_mode=pl.Buffered(1))


def _dot(a, b):
    return jnp.dot(a, b, preferred_element_type=F32)


def _rms(x):
    return x * lax.rsqrt(jnp.mean(x * x, axis=-1, keepdims=True) + NORM_EPS)


def _ada_kernel(c_ref, w_ref, b_ref, o_ref):
    c = c_ref[...]
    a = c / (1.0 + jnp.exp(-c))
    w = w_ref[0]
    a_hi = a.astype(BF16)
    a_lo = (a - a_hi.astype(F32)).astype(BF16)
    w_hi = w.astype(BF16)
    w_lo = (w - w_hi.astype(F32)).astype(BF16)
    acc = _dot(a_hi, w_hi) + _dot(a_hi, w_lo) + _dot(a_lo, w_hi)
    o_ref[0] = acc + b_ref[0]


def _ada_modulation(cvec, w_ada, b_ada):
    n_col = 6 * D_MODEL
    return pl.pallas_call(
        _ada_kernel,
        grid=(DEPTH, n_col // ADA_COLS),
        in_specs=[
            pl.BlockSpec((8, D_MODEL), lambda l, j: (0, 0)),
            pl.BlockSpec((1, D_MODEL, ADA_COLS), lambda l, j: (l, 0, j)),
            pl.BlockSpec((1, 1, ADA_COLS), lambda l, j: (l, 0, j)),
        ],
        out_specs=pl.BlockSpec((1, 8, ADA_COLS), lambda l, j: (l, 0, j)),
        out_shape=jax.ShapeDtypeStruct((DEPTH, 8, n_col), F32),
        compiler_params=_params(("arbitrary", "arbitrary")),
        name="ada_mod",
    )(cvec, w_ada, b_ada.reshape(DEPTH, 1, n_col))


def _store_heads(k_ref, k, n_heads):
    for h in range(n_heads):
        k_ref[0, h] = k[:, h * HEAD_DIM:(h + 1) * HEAD_DIM].astype(BF16)


def _store_key_norms(n_ref, k, n_heads):
    rows = []
    for h in range(8):
        if h < n_heads:
            kh = k[:, h * HEAD_DIM:(h + 1) * HEAD_DIM]
            n2 = jnp.max(jnp.sum(kh * kh, axis=1, keepdims=True), axis=0, keepdims=True)
            rows.append(jnp.broadcast_to(n2, (1, LANES)))
        else:
            rows.append(jnp.zeros((1, LANES), F32))
    n_ref[0, 0] = jnp.concatenate(rows, axis=0)


def _store_values_t(vt_ref, v, n_heads):
    vt = v.T.astype(BF16)
    pad_row = lax.broadcasted_iota(jnp.int32, (HEAD_DIM, v.shape[0]), 0)
    ones_pad = jnp.where(pad_row == 0, 1.0, 0.0).astype(BF16)
    for h in range(n_heads):
        vt_ref[0, h, 0:HEAD_DIM, :] = vt[h * HEAD_DIM:(h + 1) * HEAD_DIM]
        vt_ref[0, h, HEAD_DIM:2 * HEAD_DIM, :] = ones_pad


def _pair_swap(x):
    lane = lax.broadcasted_iota(jnp.int32, x.shape, 1)
    return jnp.where((lane & 1) == 0, pltpu.roll(x, LANES - 1, axis=1), pltpu.roll(x, 1, axis=1))


def _proj_kernel(x_ref, mod_ref, gpre_ref, vecs_ref, cos_ref, sin_ref, gmat_ref, w_ref,
                 qa_ref, ka_ref, va_ref, qb_ref, kb_ref, vb_ref, qc_ref, kc_ref, vc_ref, sg_ref,
                 na_ref, nc_ref):
    x = x_ref[0]
    h = _rms(x) * gpre_ref[...]
    h = h * (1.0 + mod_ref[0, 1:2, :]) + mod_ref[0, 0:1, :]
    hb = h.astype(BF16)
    cos = cos_ref[...]
    sin = sin_ref[...]

    ra = _dot(hb, w_ref[:, A_OFF:A_OFF + QK_W + KV_WIDTH])
    for j in range(4):
        lo = j * LANES
        xa = ra[:, lo:lo + LANES]
        roped = xa * cos + _pair_swap(xa) * sin
        if j < 3:
            qa_ref[0, lo:lo + LANES, :] = roped.T.astype(BF16)
        else:
            _store_heads(ka_ref, roped, A_KV_HEADS)
            _store_key_norms(na_ref, roped, A_KV_HEADS)
    _store_values_t(va_ref, ra[:, QK_W:QK_W + KV_WIDTH], A_KV_HEADS)

    rb = _dot(hb, w_ref[:, B_OFF:B_OFF + QK_W + KV_WIDTH])
    xb = rb[:, 0:QK_W]
    sq = xb * xb
    sq_hi = sq.astype(BF16)
    sq_lo = (sq - sq_hi.astype(F32)).astype(BF16)
    head_ms = (_dot(sq_hi, gmat_ref[...]) + _dot(sq_lo, gmat_ref[...])) * (1.0 / HEAD_DIM)
    rinv = lax.rsqrt(head_ms + NORM_EPS)
    cq, sq_t = cos * vecs_ref[0:1, :], sin * vecs_ref[1:2, :]
    ck, sk_t = cos * vecs_ref[2:3, :], sin * vecs_ref[3:4, :]
    for j in range(4):
        lo = j * LANES
        c_t, s_t = (cq, sq_t) if j < 3 else (ck, sk_t)
        xj = rb[:, lo:lo + LANES]
        roped = rinv[:, lo:lo + LANES] * (xj * c_t + _pair_swap(xj) * s_t)
        if j < 3:
            qb_ref[0, lo:lo + LANES, :] = roped.T.astype(BF16)
        else:
            _store_heads(kb_ref, roped, B_KV_HEADS)
    _store_values_t(vb_ref, rb[:, QK_W:QK_W + KV_WIDTH], B_KV_HEADS)

    rc = _dot(hb, w_ref[:, C_OFF:C_OFF + 1152])
    qc_ref[0] = rc[:, 0:Q_WIDTH].T.astype(BF16)
    _store_heads(kc_ref, rc[:, Q_WIDTH:2 * Q_WIDTH], C_HEADS)
    _store_key_norms(nc_ref, rc[:, Q_WIDTH:2 * Q_WIDTH], C_HEADS)
    _store_values_t(vc_ref, rc[:, 2 * Q_WIDTH:3 * Q_WIDTH], C_HEADS)

    for j in range(3):
        lo = G_OFF + j * D_MODEL
        g = _dot(hb, w_ref[:, lo:lo + D_MODEL])
        sg_ref[0, :, j * D_MODEL:(j + 1) * D_MODEL] = (1.0 / (1.0 + jnp.exp(-g))).astype(BF16)


def _input_projection(x, mod, gpre, vecs, cos_t, sin_t, gmat, w_ext):
    bsz, n_tok, _ = x.shape
    tm = ROW_TILE
    bf = lambda *s: jax.ShapeDtypeStruct(s, BF16)
    q_spec = pl.BlockSpec((1, Q_WIDTH, tm), lambda b, i: (b, 0, i))
    k_spec = lambda n: pl.BlockSpec((1, n, tm, HEAD_DIM), lambda b, i: (b, 0, i, 0))
    v_spec = lambda n: pl.BlockSpec((1, n, 2 * HEAD_DIM, tm), lambda b, i: (b, 0, 0, i))
    qkv_specs, qkv_shapes = [], []
    for n in (A_KV_HEADS, B_KV_HEADS, C_HEADS):
        qkv_specs += [q_spec, k_spec(n), v_spec(n)]
        qkv_shapes += [bf(bsz, Q_WIDTH, n_tok), bf(bsz, n, n_tok, HEAD_DIM),
                       bf(bsz, n, 2 * HEAD_DIM, n_tok)]
    return pl.pallas_call(
        _proj_kernel,
        grid=(bsz, n_tok // tm),
        in_specs=[
            pl.BlockSpec((1, tm, D_MODEL), lambda b, i: (b, i, 0)),
            pl.BlockSpec((1, 6, D_MODEL), lambda b, i: (b, 0, 0)),
            _const_spec((1, D_MODEL)),
            _const_spec((8, LANES)),
            pl.BlockSpec((tm, LANES), lambda b, i: (i, 0)),
            pl.BlockSpec((tm, LANES), lambda b, i: (i, 0)),
            _const_spec((QK_W, QK_W)),
            _const_spec((D_MODEL, EXT_COLS)),
        ],
        out_specs=qkv_specs + [pl.BlockSpec((1, tm, 3 * D_MODEL), lambda b, i: (b, i, 0))]
                  + [pl.BlockSpec((1, 1, 8, LANES), lambda b, i: (b, i, 0, 0))] * 2,
        out_shape=qkv_shapes + [bf(bsz, n_tok, 3 * D_MODEL)]
                  + [jax.ShapeDtypeStruct((bsz, n_tok // tm, 8, LANES), F32)] * 2,
        compiler_params=_params(("arbitrary", "arbitrary")),
        name="in_proj",
    )(x, mod, gpre, vecs, cos_t, sin_t, gmat, w_ext)


def _flash_update_t(carry, k, vt, qt, bias=None):
    m, acc = carry
    st = _dot(k, qt)
    if bias is not None:
        st = st + bias
    m_new = jnp.maximum(m, jnp.max(st, axis=0, keepdims=True))
    alpha = jnp.exp2(m - m_new)
    p = jnp.exp2(st - m_new).astype(BF16)
    return m_new, alpha * acc + _dot(vt, p)


def _flash_init_t(n_q):
    return jnp.full((1, n_q), NEG_INF, F32), jnp.zeros((2 * HEAD_DIM, n_q), F32)


def _flash_finish_t(carry, sink=None):
    m, acc = carry
    denom = acc[HEAD_DIM:HEAD_DIM + 1]
    if sink is not None:
        denom = denom + jnp.exp2(sink - m)
    return acc[0:HEAD_DIM] / denom


def _score_bound_t(kn2, qt):
    qf = qt.astype(F32)
    return 1.02 * jnp.sqrt(kn2 * jnp.sum(qf * qf, axis=0, keepdims=True))


def _softmax_pv_t(st, vt, sink=None, ref=None):
    m = jnp.max(st, axis=0, keepdims=True) if ref is None else ref
    acc = _dot(vt, jnp.exp2(st - m).astype(BF16))
    denom = acc[HEAD_DIM:HEAD_DIM + 1]
    if sink is not None:
        denom = denom + jnp.exp2(sink - m)
    return acc[0:HEAD_DIM] / denom


def _stack_heads_t(qt_ref, g, group):
    return jnp.concatenate(
        [qt_ref[0, (g * group + j) * HEAD_DIM:(g * group + j + 1) * HEAD_DIM, :]
         for j in range(group)], axis=1)


def _sink_row(sink_ref, g, group, n_q):
    return jnp.concatenate(
        [jnp.full((1, n_q), sink_ref[g * group + j], F32) for j in range(group)], axis=1)


def _attn_a_kernel(sink_ref, kn_ref, qt_ref, k0_ref, k1_ref, k2_ref, k3_ref,
                   v0_ref, v1_ref, v2_ref, v3_ref, kc_ref, vc_ref, o_ref, st_ref):
    t = pl.program_id(1)
    nt = pl.num_programs(1)
    tq = o_ref.shape[1]
    n_win = 4 * WIN_BLOCK
    group = A_Q_HEADS // A_KV_HEADS
    kj = lax.broadcasted_iota(jnp.int32, (n_win, tq), 0)
    qi = lax.broadcasted_iota(jnp.int32, (n_win, tq), 1)
    rel = kj - WIN_BLOCK - qi
    valid = (rel <= WINDOW) & (rel >= -WINDOW)
    valid = valid & ((kj >= WIN_BLOCK) | (t > 0)) & ((kj < 3 * WIN_BLOCK) | (t < nt - 1))
    mask = jnp.where(valid, 0.0, NEG_INF).astype(F32)
    mask = jnp.concatenate([mask] * group, axis=1)

    def scores_to(g):
        keys = jnp.concatenate([r[0, g] for r in (k0_ref, k1_ref, k2_ref, k3_ref, kc_ref)], axis=0)
        st = _dot(keys, _stack_heads_t(qt_ref, g, group))
        st_ref[g, 0:n_win] = st[0:n_win] + mask
        st_ref[g, n_win:] = st[n_win:]

    def attend(g, ref):
        vals_t = jnp.concatenate([r[0, g] for r in (v0_ref, v1_ref, v2_ref, v3_ref, vc_ref)], axis=1)
        og = _softmax_pv_t(st_ref[g], vals_t, sink=_sink_row(sink_ref, g, group, tq), ref=ref)
        return [og[:, j * tq:(j + 1) * tq] for j in range(group)]

    def run(refs):
        heads_t = []
        scores_to(0)
        for g in range(A_KV_HEADS):
            if g + 1 < A_KV_HEADS:
                scores_to(g + 1)
            heads_t += attend(g, refs[g])
        o_ref[0] = jnp.concatenate(heads_t, axis=0).T.astype(BF16)

    bounds = [_score_bound_t(kn_ref[0, 0, g:g + 1, 0:1], _stack_heads_t(qt_ref, g, group))
              for g in range(A_KV_HEADS)]
    worst = bounds[0]
    for b in bounds[1:]:
        worst = jnp.maximum(worst, b)
    bounded = 2.0 * jnp.max(worst) <= MAX_SAFE_SCORE_SPREAD

    @pl.when(bounded)
    def _():
        def keys_of(g):
            return jnp.concatenate([r[0, g] for r in (k0_ref, k1_ref, k2_ref, k3_ref, kc_ref)], axis=0)

        def scores_block(g, j):
            ls = slice(j * tq, (j + 1) * tq)
            st = _dot(keys_of(g), qt_ref[0, (g * group + j) * HEAD_DIM:(g * group + j + 1) * HEAD_DIM, :])
            st_ref[g, 0:n_win, ls] = st[0:n_win] + mask[:, 0:tq]
            st_ref[g, n_win:, ls] = st[n_win:]

        for j in range(group):
            scores_block(0, j)
        heads_t = []
        for g in range(A_KV_HEADS):
            vals_t = jnp.concatenate([r[0, g] for r in (v0_ref, v1_ref, v2_ref, v3_ref, vc_ref)], axis=1)
            for j in range(group):
                ls = slice(j * tq, (j + 1) * tq)
                if g + 1 < A_KV_HEADS:
                    scores_block(g + 1, j)
                ref = bounds[g][:, ls]
                acc = _dot(vals_t, jnp.exp2(st_ref[g, :, ls] - ref).astype(BF16))
                denom = acc[HEAD_DIM:HEAD_DIM + 1] + jnp.exp2(sink_ref[g * group + j] - ref)
                heads_t.append(acc[0:HEAD_DIM] / denom)
        o_ref[0] = jnp.concatenate(heads_t, axis=0).T.astype(BF16)

    @pl.when(jnp.logical_not(bounded))
    def _():
        run([None] * A_KV_HEADS)


def _attention_a(sink, kn, qt, k, vt, k_ctx, vt_ctx):
    bsz, _, n_tok = qt.shape
    tq = ROW_TILE
    nb = n_tok // WIN_BLOCK
    per = tq // WIN_BLOCK
    slot = lambda s: (lambda b, i: jnp.clip(per * i - 1 + s, 0, nb - 1))
    k_spec = lambda s: pl.BlockSpec((1, A_KV_HEADS, WIN_BLOCK, HEAD_DIM),
                                    lambda b, i: (b, 0, slot(s)(b, i), 0))
    v_spec = lambda s: pl.BlockSpec((1, A_KV_HEADS, 2 * HEAD_DIM, WIN_BLOCK),
                                    lambda b, i: (b, 0, 0, slot(s)(b, i)))
    return pl.pallas_call(
        _attn_a_kernel,
        grid=(bsz, n_tok // tq),
        in_specs=[pl.BlockSpec(memory_space=pltpu.SMEM),
                  pl.BlockSpec((1, 1, 8, LANES), lambda b, i: (b, i, 0, 0)),
                  pl.BlockSpec((1, Q_WIDTH, tq), lambda b, i: (b, 0, i))]
                 + [k_spec(s) for s in range(4)] + [v_spec(s) for s in range(4)]
                 + [pl.BlockSpec((1, A_KV_HEADS, CTX_LEN, HEAD_DIM), lambda b, i: (b, 0, 0, 0)),
                    pl.BlockSpec((1, A_KV_HEADS, 2 * HEAD_DIM, CTX_LEN), lambda b, i: (b, 0, 0, 0))],
        out_specs=pl.BlockSpec((1, tq, Q_WIDTH), lambda b, i: (b, i, 0)),
        out_shape=jax.ShapeDtypeStruct((bsz, n_tok, Q_WIDTH), BF16),
        scratch_shapes=[pltpu.VMEM((A_KV_HEADS, 4 * WIN_BLOCK + CTX_LEN,
                                    (A_Q_HEADS // A_KV_HEADS) * tq), F32)],
        compiler_params=_params(("arbitrary", "arbitrary")),
        name="attn_window",
    )(sink, kn, qt, k, k, k, k, vt, vt, vt, vt, k_ctx, vt_ctx)


def _attn_b_kernel(bound_ref, qt_ref, kc_ref, vtc_ref, k_ref, vt_ref, o_ref, st0_ref, st1_ref):
    st_ref = (st0_ref, st1_ref)
    tq = o_ref.shape[1]
    n_chunks = k_ref.shape[2] // KV_CHUNK
    group = B_Q_HEADS // B_KV_HEADS
    groups = range(B_KV_HEADS)
    qts = [_stack_heads_t(qt_ref, g, group) for g in groups]

    def chunk(c, g):
        start = pl.multiple_of(c * KV_CHUNK, KV_CHUNK)
        return k_ref[0, g, pl.ds(start, KV_CHUNK), :], vt_ref[0, g, :, pl.ds(start, KV_CHUNK)]

    def write(outs_t):
        heads_t = [og[:, j * tq:(j + 1) * tq] for og in outs_t for j in range(group)]
        o_ref[0] = jnp.concatenate(heads_t, axis=0).T.astype(BF16)

    bound = bound_ref[0]
    bounded = bound <= MAX_SAFE_SCORE_BOUND

    @pl.when(bounded)
    def _():
        keys_of = lambda c: (lambda g: chunk(c, g)[0])
        vals_of = lambda c: (lambda g: chunk(c, g)[1])

        def stage(accs, cur, next_keys, cur_vals, cur_rows=KV_CHUNK):
            out = []
            for g in groups:
                k_next = None if next_keys is None else next_keys(g)
                vt_cur = None if cur_vals is None else cur_vals(g)
                parts = []
                for j in range(group):
                    ls = slice(j * tq, (j + 1) * tq)
                    if k_next is not None:
                        st_ref[1 - cur][g, 0:k_next.shape[0], ls] = _dot(k_next, qts[g][:, ls])
                    if vt_cur is not None:
                        p = jnp.exp2(st_ref[cur][g, 0:cur_rows, ls] - bound).astype(BF16)
                        parts.append(accs[g][:, ls] + _dot(vt_cur, p))
                out.append(jnp.concatenate(parts, axis=1) if parts else accs[g])
            return tuple(out)

        def body(i, accs):
            c = 2 * i
            accs = stage(accs, 0, keys_of(c + 1), vals_of(c))
            return stage(accs, 1, keys_of(c + 2), vals_of(c + 1))

        accs = tuple(jnp.zeros((2 * HEAD_DIM, group * tq), F32) for g in groups)
        accs = stage(accs, 1, keys_of(0), None)
        accs = lax.fori_loop(0, n_chunks // 2 - 1, body, accs)
        accs = stage(accs, 0, keys_of(n_chunks - 1), vals_of(n_chunks - 2))
        accs = stage(accs, 1, lambda g: kc_ref[0, g], vals_of(n_chunks - 1))
        accs = stage(accs, 0, None, lambda g: vtc_ref[0, g], cur_rows=kc_ref.shape[2])
        write([acc[0:HEAD_DIM] / acc[HEAD_DIM:HEAD_DIM + 1] for acc in accs])

    @pl.when(jnp.logical_not(bounded))
    def _():
        carries = tuple(_flash_update_t(_flash_init_t(group * tq), kc_ref[0, g], vtc_ref[0, g], qts[g])
                        for g in groups)
        carries = lax.fori_loop(
            0, n_chunks,
            lambda c, carries: tuple(_flash_update_t(carries[g], *chunk(c, g), qts[g])
                                     for g in groups), carries)
        write([_flash_finish_t(carry) for carry in carries])


def _attention_b(bound, qt, k, vt, k_ctx, vt_ctx):
    bsz, _, n_tok = qt.shape
    tq = GLOBAL_Q_TILE
    keys = lambda n: pl.BlockSpec((1, B_KV_HEADS, n, HEAD_DIM), lambda b, i: (b, 0, 0, 0))
    vals = lambda n: pl.BlockSpec((1, B_KV_HEADS, 2 * HEAD_DIM, n), lambda b, i: (b, 0, 0, 0))
    return pl.pallas_call(
        _attn_b_kernel,
        grid=(bsz, n_tok // tq),
        in_specs=[pl.BlockSpec(memory_space=pltpu.SMEM),
                  pl.BlockSpec((1, Q_WIDTH, tq), lambda b, i: (b, 0, i)),
                  keys(CTX_LEN), vals(CTX_LEN), keys(n_tok), vals(n_tok)],
        out_specs=pl.BlockSpec((1, tq, Q_WIDTH), lambda b, i: (b, i, 0)),
        out_shape=jax.ShapeDtypeStruct((bsz, n_tok, Q_WIDTH), BF16),
        scratch_shapes=[pltpu.VMEM((B_KV_HEADS, KV_CHUNK, (B_Q_HEADS // B_KV_HEADS) * tq), F32)] * 2,
        compiler_params=_params(("arbitrary", "arbitrary")),
        name="attn_global",
    )(bound, qt, k_ctx, vt_ctx, k, vt)


def _attn_c_kernel(brange_ref, kn_ref, tab_ref, qt_ref, k0_ref, k1_ref, k2_ref,
                   v0_ref, v1_ref, v2_ref, kc_ref, vc_ref, o_ref, st_ref):
    n_win = tab_ref.shape[2]

    def scores_to(h):
        keys = jnp.concatenate([r[0, h] for r in (k0_ref, k1_ref, k2_ref, kc_ref)], axis=0)
        st = _dot(keys, qt_ref[0, h * HEAD_DIM:(h + 1) * HEAD_DIM, :])
        st_ref[h % 2, 0:n_win] = st[0:n_win] + tab_ref[0, h]
        st_ref[h % 2, n_win:] = st[n_win:]

    def attend(h, ref):
        vals_t = jnp.concatenate([r[0, h] for r in (v0_ref, v1_ref, v2_ref, vc_ref)], axis=1)
        return _softmax_pv_t(st_ref[h % 2], vals_t, ref=ref)

    def run(refs):
        heads_t = []
        scores_to(0)
        for h in range(C_HEADS):
            if h + 1 < C_HEADS:
                scores_to(h + 1)
            heads_t.append(attend(h, refs[h]))
        o_ref[0] = jnp.concatenate(heads_t, axis=0).T.astype(BF16)

    bias_hi, bias_lo = brange_ref[0], brange_ref[1]
    bounds = [_score_bound_t(kn_ref[0, 0, h:h + 1, 0:1], qt_ref[0, h * HEAD_DIM:(h + 1) * HEAD_DIM, :])
              for h in range(C_HEADS)]
    worst = bounds[0]
    for b in bounds[1:]:
        worst = jnp.maximum(worst, b)
    bounded = 2.0 * jnp.max(worst) + (bias_hi - bias_lo) <= MAX_SAFE_SCORE_SPREAD

    @pl.when(bounded)
    def _():
        tq = o_ref.shape[1]
        kv = ((k0_ref, v0_ref), (k1_ref, v1_ref), (k2_ref, v2_ref), (kc_ref, vc_ref))

        def scores_tile(h, s):
            st = _dot(kv[s][0][0, h], qt_ref[0, h * HEAD_DIM:(h + 1) * HEAD_DIM, :])
            if s < 3:
                st = st + tab_ref[0, h, s * tq:(s + 1) * tq, :]
            st_ref[h % 2, s * tq:(s + 1) * tq] = st

        for s in range(4):
            scores_tile(0, s)
        heads_t = []
        for h in range(C_HEADS):
            ref = bounds[h] + bias_hi
            acc = None
            for s in range(4):
                if h + 1 < C_HEADS:
                    scores_tile(h + 1, s)
                p = jnp.exp2(st_ref[h % 2, s * tq:(s + 1) * tq] - ref).astype(BF16)
                part = _dot(kv[s][1][0, h], p)
                acc = part if acc is None else acc + part
            heads_t.append(acc[0:HEAD_DIM] / acc[HEAD_DIM:HEAD_DIM + 1])
        o_ref[0] = jnp.concatenate(heads_t, axis=0).T.astype(BF16)

    @pl.when(jnp.logical_not(bounded))
    def _():
        run([None] * C_HEADS)


def _attention_c(bias_range, kn, tab, qt, k, vt, k_ctx, vt_ctx):
    bsz, _, n_tok = qt.shape
    tq = ROW_TILE
    nt = n_tok // tq
    slot = lambda s: (lambda b, i: jnp.clip(i - 1 + s, 0, nt - 1))
    k_spec = lambda s: pl.BlockSpec((1, C_HEADS, tq, HEAD_DIM), lambda b, i: (b, 0, slot(s)(b, i), 0))
    v_spec = lambda s: pl.BlockSpec((1, C_HEADS, 2 * HEAD_DIM, tq),
                                    lambda b, i: (b, 0, 0, slot(s)(b, i)))
    variant = lambda b, i: (jnp.where(i == 0, 0, jnp.where(i == nt - 1, 2, 1)), 0, 0, 0)
    return pl.pallas_call(
        _attn_c_kernel,
        grid=(bsz, nt),
        in_specs=[pl.BlockSpec(memory_space=pltpu.SMEM),
                  pl.BlockSpec((1, 1, 8, LANES), lambda b, i: (b, i, 0, 0)),
                  pl.BlockSpec((1, C_HEADS, 3 * tq, tq), variant),
                  pl.BlockSpec((1, Q_WIDTH, tq), lambda b, i: (b, 0, i))]
                 + [k_spec(s) for s in range(3)] + [v_spec(s) for s in range(3)]
                 + [pl.BlockSpec((1, C_HEADS, CTX_LEN, HEAD_DIM), lambda b, i: (b, 0, 0, 0)),
                    pl.BlockSpec((1, C_HEADS, 2 * HEAD_DIM, CTX_LEN), lambda b, i: (b, 0, 0, 0))],
        out_specs=pl.BlockSpec((1, tq, Q_WIDTH), lambda b, i: (b, i, 0)),
        out_shape=jax.ShapeDtypeStruct((bsz, n_tok, Q_WIDTH), BF16),
        scratch_shapes=[pltpu.VMEM((2, 3 * tq + CTX_LEN, tq), F32)],
        compiler_params=_params(("arbitrary", "arbitrary")),
        name="attn_nbr",
    )(bias_range, kn, tab, qt, k, k, k, vt, vt, vt, k_ctx, vt_ctx)


def _na_table_kernel(rowmat_ref, o_ref):
    rows = SEQ // GRID_W
    rpt = ROW_TILE // GRID_W
    nt = rows // rpt
    kh = min(NA_KH_MAX, rows)
    masked = jnp.full((GRID_W, GRID_W), NEG_INF, F32)
    for v, t in enumerate((0, 1, nt - 1)):
        for rl in range(rpt):
            r = rpt * t + rl
            rs = min(max(r - kh // 2, 0), rows - kh)
            for sk in range(3 * rpt):
                kr = rpt * (t - 1) + sk
                blk = rowmat_ref[0, kr - r + NA_KH_MAX - 1] if rs <= kr < rs + kh else masked
                o_ref[v, 0, sk * GRID_W:(sk + 1) * GRID_W, rl * GRID_W:(rl + 1) * GRID_W] = blk


def _neighbourhood_tables(rpb):
    c = np.arange(GRID_W)
    ws = np.clip(c - NA_KW // 2, 0, GRID_W - NA_KW)
    valid = (c[None, :] >= ws[:, None]) & (c[None, :] < ws[:, None] + NA_KW)
    coff = np.clip(c[None, :] - c[:, None] + NA_KW - 1, 0, 2 * NA_KW - 2)
    onehot = (coff[:, :, None] == np.arange(2 * NA_KW - 1)).astype(np.float32)
    rowmat = jnp.einsum("hij,ckj->hikc", rpb * LOG2_E, onehot, precision=lax.Precision.HIGHEST)
    rowmat = jnp.where(valid.T, rowmat, NEG_INF).astype(F32)
    n_off = 2 * NA_KH_MAX - 1
    return pl.pallas_call(
        _na_table_kernel,
        grid=(C_HEADS,),
        in_specs=[pl.BlockSpec((1, n_off, GRID_W, GRID_W), lambda h: (h, 0, 0, 0))],
        out_specs=pl.BlockSpec((3, 1, 3 * ROW_TILE, ROW_TILE), lambda h: (0, h, 0, 0)),
        out_shape=jax.ShapeDtypeStruct((3, C_HEADS, 3 * ROW_TILE, ROW_TILE), F32),
        compiler_params=_params(("arbitrary",)),
        name="na_tables",
    )(rowmat)


def _attn_ctx_kernel(sink_ref, qa_ref, ka_ref, va_ref, qb_ref, kb_ref, vb_ref,
                     qc_ref, kc_ref, vc_ref, oa_ref, ob_ref, oc_ref):
    group = A_Q_HEADS // A_KV_HEADS
    n_q = oa_ref.shape[1]
    outs = ([], [], [])
    for h in range(A_Q_HEADS):
        g = h // group
        hs = slice(h * HEAD_DIM, (h + 1) * HEAD_DIM)
        init = _flash_init_t(n_q)
        outs[0].append(_flash_finish_t(
            _flash_update_t(init, ka_ref[0, g], va_ref[0, g], qa_ref[0, hs, :]),
            sink=jnp.full((1, n_q), sink_ref[h], F32)))
        outs[1].append(_flash_finish_t(
            _flash_update_t(init, kb_ref[0, g], vb_ref[0, g], qb_ref[0, hs, :])))
        outs[2].append(_flash_finish_t(
            _flash_update_t(init, kc_ref[0, h], vc_ref[0, h], qc_ref[0, hs, :])))
    for o_ref, heads_t in zip((oa_ref, ob_ref, oc_ref), outs):
        o_ref[0] = jnp.concatenate(heads_t, axis=0).T.astype(BF16)


def _attention_ctx(sink, qa, ka, va, qb, kb, vb, qc, kc, vc):
    bsz = qa.shape[0]
    q_spec = pl.BlockSpec((1, Q_WIDTH, CTX_LEN), lambda b: (b, 0, 0))
    k_spec = lambda n: pl.BlockSpec((1, n, CTX_LEN, HEAD_DIM), lambda b: (b, 0, 0, 0))
    v_spec = lambda n: pl.BlockSpec((1, n, 2 * HEAD_DIM, CTX_LEN), lambda b: (b, 0, 0, 0))
    out = jax.ShapeDtypeStruct((bsz, CTX_LEN, Q_WIDTH), BF16)
    specs = [pl.BlockSpec(memory_space=pltpu.SMEM)]
    for n in (A_KV_HEADS, B_KV_HEADS, C_HEADS):
        specs += [q_spec, k_spec(n), v_spec(n)]
    return pl.pallas_call(
        _attn_ctx_kernel,
        grid=(bsz,),
        in_specs=specs,
        out_specs=[pl.BlockSpec((1, CTX_LEN, Q_WIDTH), lambda b: (b, 0, 0))] * 3,
        out_shape=[out] * 3,
        compiler_params=_params(("arbitrary",)),
        name="attn_ctx",
    )(sink, qa, ka, va, qb, kb, vb, qc, kc, vc)


def _post_kernel(x_ref, oa_ref, ob_ref, oc_ref, sg_ref, mod_ref, norms_ref,
                 wa_ref, wb_ref, wc_ref, wo_ref, w1_ref, w2_ref, o_ref):
    x = x_ref[0]
    merged = None
    for j, (o_r, w_r) in enumerate(((oa_ref, wa_ref), (ob_ref, wb_ref), (oc_ref, wc_ref))):
        gate = sg_ref[0, :, j * D_MODEL:(j + 1) * D_MODEL].astype(F32)
        term = gate * _dot(o_r[0], w_r[...])
        merged = term if merged is None else merged + term
    y = _dot(merged.astype(BF16), wo_ref[...])
    x1 = x + mod_ref[0, 2:3, :] * (_rms(y) * norms_ref[0:1, :])
    h2 = _rms(x1) * norms_ref[1:2, :]
    h2 = (h2 * (1.0 + mod_ref[0, 4:5, :]) + mod_ref[0, 3:4, :]).astype(BF16)
    z = None
    n_chunk = MLP_HIDDEN // D_MODEL
    for j in range(n_chunk):
        cs = slice(j * D_MODEL, (j + 1) * D_MODEL)
        a = jnp.maximum(_dot(h2, w1_ref[:, cs]), 0.0)
        zj = _dot((a * a).astype(BF16), w2_ref[cs, :])
        z = zj if z is None else z + zj
    o_ref[0] = x1 + mod_ref[0, 5:6, :] * (_rms(z) * norms_ref[2:3, :])


def _post_block(x, o_a, o_b, o_c, sg, mod, norms, wa, wb, wc, wo, w1, w2):
    bsz, n_tok, _ = x.shape
    tm = min(DENSE_ROW_TILE, n_tok)
    row = lambda w: pl.BlockSpec((1, tm, w), lambda b, i: (b, i, 0))
    return pl.pallas_call(
        _post_kernel,
        grid=(bsz, n_tok // tm),
        in_specs=[row(D_MODEL), row(Q_WIDTH), row(Q_WIDTH), row(Q_WIDTH), row(3 * D_MODEL),
                  pl.BlockSpec((1, 6, D_MODEL), lambda b, i: (b, 0, 0)),
                  _const_spec((8, D_MODEL)),
                  _const_spec((Q_WIDTH, D_MODEL)), _const_spec((Q_WIDTH, D_MODEL)),
                  _const_spec((Q_WIDTH, D_MODEL)), _const_spec((D_MODEL, D_MODEL)),
                  _const_spec((D_MODEL, MLP_HIDDEN)), _const_spec((MLP_HIDDEN, D_MODEL))],
        out_specs=row(D_MODEL),
        out_shape=jax.ShapeDtypeStruct((bsz, n_tok, D_MODEL), F32),
        compiler_params=_params(("arbitrary", "arbitrary")),
        name="post_mlp",
    )(x, o_a, o_b, o_c, sg, mod, norms, wa, wb, wc, wo, w1, w2)


def _rope_tables(n_tok):
    pos = jnp.arange(n_tok)
    row = (pos // GRID_W).astype(F32)
    col = (pos % GRID_W).astype(F32)
    n_freq = HEAD_DIM // 4
    freqs = ROPE_THETA ** (-jnp.arange(n_freq, dtype=F32) / n_freq)
    ang = jnp.concatenate([row[:, None] * freqs, col[:, None] * freqs], axis=-1)
    cos = jnp.repeat(jnp.cos(ang), 2, axis=-1)
    sign = jnp.tile(jnp.array([-1.0, 1.0], F32), HEAD_DIM // 2)
    sin = jnp.repeat(jnp.sin(ang), 2, axis=-1) * sign
    return jnp.tile(cos, (1, LANES // HEAD_DIM)), jnp.tile(sin, (1, LANES // HEAD_DIM))


def _extended_w_in(w_in_l):
    sizes = (Q_WIDTH, KV_WIDTH, KV_WIDTH, Q_WIDTH, KV_WIDTH, KV_WIDTH, Q_WIDTH, Q_WIDTH, Q_WIDTH)
    parts, start = [], 0
    for n in sizes:
        parts.append(w_in_l[:, start:start + n])
        start += n
    qa, ka, va, qb, kb, vb, qc, kc, vc = parts
    gates = w_in_l[:, start:]
    cols = [qa * Q_SCALE, ka, va, qb, kb, vb, qc * Q_SCALE, kc, vc, gates]
    return jnp.concatenate(cols, axis=1).astype(BF16)


def _window_key_norms(kn, kn_ctx):
    prev = jnp.concatenate([kn[:, :1], kn[:, :-1]], axis=1)
    nxt = jnp.concatenate([kn[:, 1:], kn[:, -1:]], axis=1)
    return jnp.maximum(jnp.maximum(prev, kn), jnp.maximum(nxt, kn_ctx))


def kernel(x, c, ctx, c_ctx, w_ada, b_ada, norm_mix_pre, norm_mix_post, w_in, sink_a, qnorm_b,
           knorm_b, rpb_c, w_br_a, w_br_b, w_br_c, w_out, norm_mlp_pre, norm_mlp_post,
           w_mlp_in, w_mlp_out):
    bsz, seq, _ = x.shape
    n_ctx = ctx.shape[1]
    cvec = jnp.concatenate([c, c_ctx[None, :], jnp.zeros((8 - bsz - 1, D_MODEL), F32)], axis=0)
    mod_all = _ada_modulation(cvec, w_ada, b_ada)

    cos_lat, sin_lat = _rope_tables(seq)
    cos_ctx = jnp.ones((n_ctx, LANES), F32)
    sin_ctx = jnp.zeros((n_ctx, LANES), F32)
    head_id = jnp.arange(QK_W) // HEAD_DIM
    gmat = (head_id[:, None] == head_id[None, :]).astype(BF16)
    tile2 = lambda g: jnp.tile(g, LANES // HEAD_DIM)
    swap1 = lambda g: g[jnp.arange(HEAD_DIM) ^ 1]

    x_lat, x_ctx = x, ctx
    for l in range(DEPTH):
        last = l == DEPTH - 1
        mod_lat = mod_all[l, :bsz].reshape(bsz, 6, D_MODEL)
        mod_ctx = jnp.broadcast_to(mod_all[l, bsz].reshape(1, 6, D_MODEL), (bsz, 6, D_MODEL))
        gq, gk = qnorm_b[l], knorm_b[l]
        vecs = jnp.stack([tile2(gq) * Q_SCALE, tile2(swap1(gq)) * Q_SCALE,
                          tile2(gk), tile2(swap1(gk))]
                         + [jnp.zeros((LANES,), F32)] * 4)
        score_bound = (1.02 * HEAD_DIM * Q_SCALE * jnp.max(jnp.abs(gq)) * jnp.max(jnp.abs(gk))
                       ).reshape(1).astype(F32)
        gpre = norm_mix_pre[l][None, :]
        w_ext = _extended_w_in(w_in[l])
        norms = jnp.stack([norm_mix_post[l], norm_mlp_pre[l], norm_mlp_post[l]]
                          + [jnp.zeros((D_MODEL,), F32)] * 5)
        wa, wb, wc = (w.astype(BF16) for w in (w_br_a[l], w_br_b[l], w_br_c[l]))
        wo = w_out[l].astype(BF16)
        w1 = w_mlp_in[l].astype(BF16)
        w2 = w_mlp_out[l].astype(BF16)
        tabs = _neighbourhood_tables(rpb_c[l])
        sink = sink_a[l] * LOG2_E

        proj_lat = _input_projection(x_lat, mod_lat, gpre, vecs, cos_lat, sin_lat, gmat, w_ext)
        proj_ctx = _input_projection(x_ctx, mod_ctx, gpre, vecs, cos_ctx, sin_ctx, gmat, w_ext)
        qa, ka, va, qb, kb, vb, qc, kc, vc, sg, na, nc = proj_lat
        qa_c, ka_c, va_c, qb_c, kb_c, vb_c, qc_c, kc_c, vc_c, sg_c, na_c, nc_c = proj_ctx

        o_a = _attention_a(sink, _window_key_norms(na, na_c), qa, ka, va, ka_c, va_c)
        o_b = _attention_b(score_bound, qb, kb, vb, kb_c, vb_c)
        bias = rpb_c[l] * LOG2_E
        bias_range = jnp.stack([jnp.maximum(jnp.max(bias), 0.0), jnp.minimum(jnp.min(bias), 0.0)])
        o_c = _attention_c(bias_range, _window_key_norms(nc, nc_c), tabs, qc, kc, vc, kc_c, vc_c)
        x_lat = _post_block(x_lat, o_a, o_b, o_c, sg, mod_lat, norms, wa, wb, wc, wo, w1, w2)
        if not last:
            o_a_c, o_b_c, o_c_c = _attention_ctx(sink, qa_c, ka_c, va_c, qb_c, kb_c, vb_c,
                                                 qc_c, kc_c, vc_c)
            x_ctx = _post_block(x_ctx, o_a_c, o_b_c, o_c_c, sg_c, mod_ctx, norms,
                                wa, wb, wc, wo, w1, w2)
    return x_lat
```

```python
import jax
import jax.numpy as jnp
import numpy as np
from jax import lax
from jax.experimental import pallas as pl
from jax.experimental.pallas import tpu as pltpu

D_MODEL = 1024
SEQ = 8192
DEPTH = 2
GRID_W = 64
CTX_LEN = 256
HEAD_DIM = 64
A_Q_HEADS = 6
A_KV_HEADS = 2
WINDOW = 128
B_Q_HEADS = 6
B_KV_HEADS = 2
C_HEADS = 6
NA_KH_MAX = 8
NA_KW = 16
Q_WIDTH = 6 * HEAD_DIM
KV_WIDTH = 2 * HEAD_DIM
MLP_HIDDEN = 4 * D_MODEL
ROPE_THETA = 10000.0
NORM_EPS = 1e-6
NEG_INF = -1e30
SM_SCALE = HEAD_DIM ** -0.5
LOG2_E = 1.4426950408889634
Q_SCALE = SM_SCALE * LOG2_E

V7X_VMEM_LIMIT_BYTES = 56 * 1024 * 1024
LANES = 128
ROW_TILE = 256
DENSE_ROW_TILE = 512
WIN_BLOCK = 128
KV_CHUNK = 512
GLOBAL_Q_TILE = 512
ADA_COLS = 1536
MAX_SAFE_SCORE_BOUND = 60.0
MAX_SAFE_SCORE_SPREAD = 2.0 * MAX_SAFE_SCORE_BOUND

BF16 = jnp.bfloat16
F32 = jnp.float32

A_OFF = 0
B_OFF = A_OFF + 640
C_OFF = B_OFF + 640
G_OFF = C_OFF + 1152
EXT_COLS = G_OFF + 3 * D_MODEL
QK_W = Q_WIDTH + KV_WIDTH


def _params(semantics):
    return pltpu.CompilerParams(dimension_semantics=semantics,
                                vmem_limit_bytes=V7X_VMEM_LIMIT_BYTES)


def _const_spec(shape):
    nd = len(shape)
    return pl.BlockSpec(shape, lambda *_: (0,) * nd, pipeline_mode=pl.Buffered(1))


def _dot(a, b):
    return jnp.dot(a, b, preferred_element_type=F32)


def _rms(x):
    return x * lax.rsqrt(jnp.mean(x * x, axis=-1, keepdims=True) + NORM_EPS)


def _ada_kernel(c_ref, w_ref, b_ref, o_ref):
    c = c_ref[...]
    a = c / (1.0 + jnp.exp(-c))
    w = w_ref[0]
    a_hi = a.astype(BF16)
    a_lo = (a - a_hi.astype(F32)).astype(BF16)
    w_hi = w.astype(BF16)
    w_lo = (w - w_hi.astype(F32)).astype(BF16)
    acc = _dot(a_hi, w_hi) + _dot(a_hi, w_lo) + _dot(a_lo, w_hi)
    o_ref[0] = acc + b_ref[0]


def _ada_modulation(cvec, w_ada, b_ada):
    n_col = 6 * D_MODEL
    return pl.pallas_call(
        _ada_kernel,
        grid=(DEPTH, n_col // ADA_COLS),
        in_specs=[
            pl.BlockSpec((8, D_MODEL), lambda l, j: (0, 0)),
            pl.BlockSpec((1, D_MODEL, ADA_COLS), lambda l, j: (l, 0, j)),
            pl.BlockSpec((1, 1, ADA_COLS), lambda l, j: (l, 0, j)),
        ],
        out_specs=pl.BlockSpec((1, 8, ADA_COLS), lambda l, j: (l, 0, j)),
        out_shape=jax.ShapeDtypeStruct((DEPTH, 8, n_col), F32),
        compiler_params=_params(("arbitrary", "arbitrary")),
        name="ada_mod",
    )(cvec, w_ada, b_ada.reshape(DEPTH, 1, n_col))


def _store_heads(k_ref, k, n_heads):
    for h in range(n_heads):
        k_ref[0, h] = k[:, h * HEAD_DIM:(h + 1) * HEAD_DIM].astype(BF16)


def _store_key_norms(n_ref, k, n_heads):
    rows = []
    for h in range(8):
        if h < n_heads:
            kh = k[:, h * HEAD_DIM:(h + 1) * HEAD_DIM]
            n2 = jnp.max(jnp.sum(kh * kh, axis=1, keepdims=True), axis=0, keepdims=True)
            rows.append(jnp.broadcast_to(n2, (1, LANES)))
        else:
            rows.append(jnp.zeros((1, LANES), F32))
    n_ref[0, 0] = jnp.concatenate(rows, axis=0)


def _store_values_t(vt_ref, v, n_heads):
    vt = v.T.astype(BF16)
    pad_row = lax.broadcasted_iota(jnp.int32, (HEAD_DIM, v.shape[0]), 0)
    ones_pad = jnp.where(pad_row == 0, 1.0, 0.0).astype(BF16)
    for h in range(n_heads):
        vt_ref[0, h, 0:HEAD_DIM, :] = vt[h * HEAD_DIM:(h + 1) * HEAD_DIM]
        vt_ref[0, h, HEAD_DIM:2 * HEAD_DIM, :] = ones_pad


def _pair_swap(x):
    lane = lax.broadcasted_iota(jnp.int32, x.shape, 1)
    return jnp.where((lane & 1) == 0, pltpu.roll(x, LANES - 1, axis=1), pltpu.roll(x, 1, axis=1))


def _proj_kernel(x_ref, mod_ref, gpre_ref, vecs_ref, cos_ref, sin_ref, gmat_ref, w_ref,
                 qa_ref, ka_ref, va_ref, qb_ref, kb_ref, vb_ref, qc_ref, kc_ref, vc_ref, sg_ref,
                 na_ref, nc_ref):
    x = x_ref[0]
    h = _rms(x) * gpre_ref[...]
    h = h * (1.0 + mod_ref[0, 1:2, :]) + mod_ref[0, 0:1, :]
    hb = h.astype(BF16)
    cos = cos_ref[...]
    sin = sin_ref[...]

    ra = _dot(hb, w_ref[:, A_OFF:A_OFF + QK_W + KV_WIDTH])
    for j in range(4):
        lo = j * LANES
        xa = ra[:, lo:lo + LANES]
        roped = xa * cos + _pair_swap(xa) * sin
        if j < 3:
            qa_ref[0, lo:lo + LANES, :] = (roped * Q_SCALE).T.astype(BF16)
        else:
            _store_heads(ka_ref, roped, A_KV_HEADS)
            _store_key_norms(na_ref, roped, A_KV_HEADS)
    _store_values_t(va_ref, ra[:, QK_W:QK_W + KV_WIDTH], A_KV_HEADS)

    rb = _dot(hb, w_ref[:, B_OFF:B_OFF + QK_W + KV_WIDTH])
    xb = rb[:, 0:QK_W]
    sq = xb * xb
    sq_hi = sq.astype(BF16)
    sq_lo = (sq - sq_hi.astype(F32)).astype(BF16)
    head_ms = (_dot(sq_hi, gmat_ref[...]) + _dot(sq_lo, gmat_ref[...])) * (1.0 / HEAD_DIM)
    rinv = lax.rsqrt(head_ms + NORM_EPS)
    cq, sq_t = cos * vecs_ref[0:1, :], sin * vecs_ref[1:2, :]
    ck, sk_t = cos * vecs_ref[2:3, :], sin * vecs_ref[3:4, :]
    for j in range(4):
        lo = j * LANES
        c_t, s_t = (cq, sq_t) if j < 3 else (ck, sk_t)
        xj = rb[:, lo:lo + LANES]
        roped = rinv[:, lo:lo + LANES] * (xj * c_t + _pair_swap(xj) * s_t)
        if j < 3:
            qb_ref[0, lo:lo + LANES, :] = roped.T.astype(BF16)
        else:
            _store_heads(kb_ref, roped, B_KV_HEADS)
    _store_values_t(vb_ref, rb[:, QK_W:QK_W + KV_WIDTH], B_KV_HEADS)

    rc = _dot(hb, w_ref[:, C_OFF:C_OFF + 1152])
    qc_ref[0] = (rc[:, 0:Q_WIDTH] * Q_SCALE).T.astype(BF16)
    _store_heads(kc_ref, rc[:, Q_WIDTH:2 * Q_WIDTH], C_HEADS)
    _store_key_norms(nc_ref, rc[:, Q_WIDTH:2 * Q_WIDTH], C_HEADS)
    _store_values_t(vc_ref, rc[:, 2 * Q_WIDTH:3 * Q_WIDTH], C_HEADS)

    for j in range(3):
        lo = G_OFF + j * D_MODEL
        g = _dot(hb, w_ref[:, lo:lo + D_MODEL])
        sg_ref[0, :, j * D_MODEL:(j + 1) * D_MODEL] = (1.0 / (1.0 + jnp.exp(-g))).astype(BF16)


def _input_projection(x, mod, gpre, vecs, cos_t, sin_t, gmat, w_ext):
    bsz, n_tok, _ = x.shape
    tm = ROW_TILE
    bf = lambda *s: jax.ShapeDtypeStruct(s, BF16)
    q_spec = pl.BlockSpec((1, Q_WIDTH, tm), lambda b, i: (b, 0, i))
    k_spec = lambda n: pl.BlockSpec((1, n, tm, HEAD_DIM), lambda b, i: (b, 0, i, 0))
    v_spec = lambda n: pl.BlockSpec((1, n, 2 * HEAD_DIM, tm), lambda b, i: (b, 0, 0, i))
    qkv_specs, qkv_shapes = [], []
    for n in (A_KV_HEADS, B_KV_HEADS, C_HEADS):
        qkv_specs += [q_spec, k_spec(n), v_spec(n)]
        qkv_shapes += [bf(bsz, Q_WIDTH, n_tok), bf(bsz, n, n_tok, HEAD_DIM),
                       bf(bsz, n, 2 * HEAD_DIM, n_tok)]
    return pl.pallas_call(
        _proj_kernel,
        grid=(bsz, n_tok // tm),
        in_specs=[
            pl.BlockSpec((1, tm, D_MODEL), lambda b, i: (b, i, 0)),
            pl.BlockSpec((1, 6, D_MODEL), lambda b, i: (b, 0, 0)),
            _const_spec((1, D_MODEL)),
            _const_spec((8, LANES)),
            pl.BlockSpec((tm, LANES), lambda b, i: (i, 0)),
            pl.BlockSpec((tm, LANES), lambda b, i: (i, 0)),
            _const_spec((QK_W, QK_W)),
            _const_spec((D_MODEL, EXT_COLS)),
        ],
        out_specs=qkv_specs + [pl.BlockSpec((1, tm, 3 * D_MODEL), lambda b, i: (b, i, 0))]
                  + [pl.BlockSpec((1, 1, 8, LANES), lambda b, i: (b, i, 0, 0))] * 2,
        out_shape=qkv_shapes + [bf(bsz, n_tok, 3 * D_MODEL)]
                  + [jax.ShapeDtypeStruct((bsz, n_tok // tm, 8, LANES), F32)] * 2,
        compiler_params=_params(("arbitrary", "arbitrary")),
        name="in_proj",
    )(x, mod, gpre, vecs, cos_t, sin_t, gmat, w_ext)


def _flash_update_t(carry, k, vt, qt, bias=None):
    m, acc = carry
    st = _dot(k, qt)
    if bias is not None:
        st = st + bias
    m_new = jnp.maximum(m, jnp.max(st, axis=0, keepdims=True))
    alpha = jnp.exp2(m - m_new)
    p = jnp.exp2(st - m_new).astype(BF16)
    return m_new, alpha * acc + _dot(vt, p)


def _flash_init_t(n_q):
    return jnp.full((1, n_q), NEG_INF, F32), jnp.zeros((2 * HEAD_DIM, n_q), F32)


def _flash_finish_t(carry, sink=None):
    m, acc = carry
    denom = acc[HEAD_DIM:HEAD_DIM + 1]
    if sink is not None:
        denom = denom + jnp.exp2(sink - m)
    return acc[0:HEAD_DIM] / denom


def _score_bound_t(kn2, qt):
    qf = qt.astype(F32)
    return 1.02 * jnp.sqrt(kn2 * jnp.sum(qf * qf, axis=0, keepdims=True))


def _softmax_pv_t(st, vt, sink=None, ref=None):
    m = jnp.max(st, axis=0, keepdims=True) if ref is None else ref
    acc = _dot(vt, jnp.exp2(st - m).astype(BF16))
    denom = acc[HEAD_DIM:HEAD_DIM + 1]
    if sink is not None:
        denom = denom + jnp.exp2(sink - m)
    return acc[0:HEAD_DIM] / denom


def _stack_heads_t(qt_ref, g, group):
    return jnp.concatenate(
        [qt_ref[0, (g * group + j) * HEAD_DIM:(g * group + j + 1) * HEAD_DIM, :]
         for j in range(group)], axis=1)


def _sink_row(sink_ref, g, group, n_q):
    return jnp.concatenate(
        [jnp.full((1, n_q), sink_ref[g * group + j], F32) for j in range(group)], axis=1)


def _attn_a_kernel(sink_ref, kn_ref, qt_ref, k0_ref, k1_ref, k2_ref, k3_ref,
                   v0_ref, v1_ref, v2_ref, v3_ref, kc_ref, vc_ref, o_ref, st_ref):
    t = pl.program_id(1)
    nt = pl.num_programs(1)
    tq = o_ref.shape[1]
    n_win = 4 * WIN_BLOCK
    group = A_Q_HEADS // A_KV_HEADS
    kj = lax.broadcasted_iota(jnp.int32, (n_win, tq), 0)
    qi = lax.broadcasted_iota(jnp.int32, (n_win, tq), 1)
    rel = kj - WIN_BLOCK - qi
    valid = (rel <= WINDOW) & (rel >= -WINDOW)
    valid = valid & ((kj >= WIN_BLOCK) | (t > 0)) & ((kj < 3 * WIN_BLOCK) | (t < nt - 1))
    mask = jnp.where(valid, 0.0, NEG_INF).astype(F32)
    mask = jnp.concatenate([mask] * group, axis=1)

    def scores_to(g):
        keys = jnp.concatenate([r[0, g] for r in (k0_ref, k1_ref, k2_ref, k3_ref, kc_ref)], axis=0)
        st = _dot(keys, _stack_heads_t(qt_ref, g, group))
        st_ref[g, 0:n_win] = st[0:n_win] + mask
        st_ref[g, n_win:] = st[n_win:]

    def attend(g, ref):
        vals_t = jnp.concatenate([r[0, g] for r in (v0_ref, v1_ref, v2_ref, v3_ref, vc_ref)], axis=1)
        og = _softmax_pv_t(st_ref[g], vals_t, sink=_sink_row(sink_ref, g, group, tq), ref=ref)
        return [og[:, j * tq:(j + 1) * tq] for j in range(group)]

    def run(refs):
        heads_t = []
        scores_to(0)
        for g in range(A_KV_HEADS):
            if g + 1 < A_KV_HEADS:
                scores_to(g + 1)
            heads_t += attend(g, refs[g])
        o_ref[0] = jnp.concatenate(heads_t, axis=0).T.astype(BF16)

    bounds = [_score_bound_t(kn_ref[0, 0, g:g + 1, 0:1], _stack_heads_t(qt_ref, g, group))
              for g in range(A_KV_HEADS)]
    worst = bounds[0]
    for b in bounds[1:]:
        worst = jnp.maximum(worst, b)
    bounded = 2.0 * jnp.max(worst) <= MAX_SAFE_SCORE_SPREAD

    @pl.when(bounded)
    def _():
        def keys_of(g):
            return jnp.concatenate([r[0, g] for r in (k0_ref, k1_ref, k2_ref, k3_ref, kc_ref)], axis=0)

        def scores_block(g, j):
            ls = slice(j * tq, (j + 1) * tq)
            st = _dot(keys_of(g), qt_ref[0, (g * group + j) * HEAD_DIM:(g * group + j + 1) * HEAD_DIM, :])
            st_ref[g, 0:n_win, ls] = st[0:n_win] + mask[:, 0:tq]
            st_ref[g, n_win:, ls] = st[n_win:]

        for j in range(group):
            scores_block(0, j)
        heads_t = []
        for g in range(A_KV_HEADS):
            vals_t = jnp.concatenate([r[0, g] for r in (v0_ref, v1_ref, v2_ref, v3_ref, vc_ref)], axis=1)
            for j in range(group):
                ls = slice(j * tq, (j + 1) * tq)
                if g + 1 < A_KV_HEADS:
                    scores_block(g + 1, j)
                ref = bounds[g][:, ls]
                acc = _dot(vals_t, jnp.exp2(st_ref[g, :, ls] - ref).astype(BF16))
                denom = acc[HEAD_DIM:HEAD_DIM + 1] + jnp.exp2(sink_ref[g * group + j] - ref)
                heads_t.append(acc[0:HEAD_DIM] / denom)
        o_ref[0] = jnp.concatenate(heads_t, axis=0).T.astype(BF16)

    @pl.when(jnp.logical_not(bounded))
    def _():
        run([None] * A_KV_HEADS)


def _attention_a(sink, kn, qt, k, vt, k_ctx, vt_ctx):
    bsz, _, n_tok = qt.shape
    tq = ROW_TILE
    nb = n_tok // WIN_BLOCK
    per = tq // WIN_BLOCK
    slot = lambda s: (lambda b, i: jnp.clip(per * i - 1 + s, 0, nb - 1))
    k_spec = lambda s: pl.BlockSpec((1, A_KV_HEADS, WIN_BLOCK, HEAD_DIM),
                                    lambda b, i: (b, 0, slot(s)(b, i), 0))
    v_spec = lambda s: pl.BlockSpec((1, A_KV_HEADS, 2 * HEAD_DIM, WIN_BLOCK),
                                    lambda b, i: (b, 0, 0, slot(s)(b, i)))
    return pl.pallas_call(
        _attn_a_kernel,
        grid=(bsz, n_tok // tq),
        in_specs=[pl.BlockSpec(memory_space=pltpu.SMEM),
                  pl.BlockSpec((1, 1, 8, LANES), lambda b, i: (b, i, 0, 0)),
                  pl.BlockSpec((1, Q_WIDTH, tq), lambda b, i: (b, 0, i))]
                 + [k_spec(s) for s in range(4)] + [v_spec(s) for s in range(4)]
                 + [pl.BlockSpec((1, A_KV_HEADS, CTX_LEN, HEAD_DIM), lambda b, i: (b, 0, 0, 0)),
                    pl.BlockSpec((1, A_KV_HEADS, 2 * HEAD_DIM, CTX_LEN), lambda b, i: (b, 0, 0, 0))],
        out_specs=pl.BlockSpec((1, tq, Q_WIDTH), lambda b, i: (b, i, 0)),
        out_shape=jax.ShapeDtypeStruct((bsz, n_tok, Q_WIDTH), BF16),
        scratch_shapes=[pltpu.VMEM((A_KV_HEADS, 4 * WIN_BLOCK + CTX_LEN,
                                    (A_Q_HEADS // A_KV_HEADS) * tq), F32)],
        compiler_params=_params(("arbitrary", "arbitrary")),
        name="attn_window",
    )(sink, kn, qt, k, k, k, k, vt, vt, vt, vt, k_ctx, vt_ctx)


def _attn_b_kernel(bound_ref, qt_ref, kc_ref, vtc_ref, k_ref, vt_ref, o_ref, st0_ref, st1_ref):
    st_ref = (st0_ref, st1_ref)
    tq = o_ref.shape[1]
    n_chunks = k_ref.shape[2] // KV_CHUNK
    group = B_Q_HEADS // B_KV_HEADS
    groups = range(B_KV_HEADS)
    qts = [_stack_heads_t(qt_ref, g, group) for g in groups]

    def chunk(c, g):
        start = pl.multiple_of(c * KV_CHUNK, KV_CHUNK)
        return k_ref[0, g, pl.ds(start, KV_CHUNK), :], vt_ref[0, g, :, pl.ds(start, KV_CHUNK)]

    def write(outs_t):
        heads_t = [og[:, j * tq:(j + 1) * tq] for og in outs_t for j in range(group)]
        o_ref[0] = jnp.concatenate(heads_t, axis=0).T.astype(BF16)

    bound = bound_ref[0]
    bounded = bound <= MAX_SAFE_SCORE_BOUND

    @pl.when(bounded)
    def _():
        keys_of = lambda c: (lambda g: chunk(c, g)[0])
        vals_of = lambda c: (lambda g: chunk(c, g)[1])

        def stage(accs, cur, next_keys, cur_vals, cur_rows=KV_CHUNK):
            out = []
            for g in groups:
                k_next = None if next_keys is None else next_keys(g)
                vt_cur = None if cur_vals is None else cur_vals(g)
                parts = []
                for j in range(group):
                    ls = slice(j * tq, (j + 1) * tq)
                    if k_next is not None:
                        st_ref[1 - cur][g, 0:k_next.shape[0], ls] = _dot(k_next, qts[g][:, ls])
                    if vt_cur is not None:
                        p = jnp.exp2(st_ref[cur][g, 0:cur_rows, ls] - bound).astype(BF16)
                        parts.append(accs[g][:, ls] + _dot(vt_cur, p))
                out.append(jnp.concatenate(parts, axis=1) if parts else accs[g])
            return tuple(out)

        def body(i, accs):
            c = 2 * i
            accs = stage(accs, 0, keys_of(c + 1), vals_of(c))
            return stage(accs, 1, keys_of(c + 2), vals_of(c + 1))

        accs = tuple(jnp.zeros((2 * HEAD_DIM, group * tq), F32) for g in groups)
        accs = stage(accs, 1, keys_of(0), None)
        accs = lax.fori_loop(0, n_chunks // 2 - 1, body, accs)
        accs = stage(accs, 0, keys_of(n_chunks - 1), vals_of(n_chunks - 2))
        accs = stage(accs, 1, lambda g: kc_ref[0, g], vals_of(n_chunks - 1))
        accs = stage(accs, 0, None, lambda g: vtc_ref[0, g], cur_rows=kc_ref.shape[2])
        write([acc[0:HEAD_DIM] / acc[HEAD_DIM:HEAD_DIM + 1] for acc in accs])

    @pl.when(jnp.logical_not(bounded))
    def _():
        carries = tuple(_flash_update_t(_flash_init_t(group * tq), kc_ref[0, g], vtc_ref[0, g], qts[g])
                        for g in groups)
        carries = lax.fori_loop(
            0, n_chunks,
            lambda c, carries: tuple(_flash_update_t(carries[g], *chunk(c, g), qts[g])
                                     for g in groups), carries)
        write([_flash_finish_t(carry) for carry in carries])


def _attention_b(bound, qt, k, vt, k_ctx, vt_ctx):
    bsz, _, n_tok = qt.shape
    tq = GLOBAL_Q_TILE
    keys = lambda n: pl.BlockSpec((1, B_KV_HEADS, n, HEAD_DIM), lambda b, i: (b, 0, 0, 0))
    vals = lambda n: pl.BlockSpec((1, B_KV_HEADS, 2 * HEAD_DIM, n), lambda b, i: (b, 0, 0, 0))
    return pl.pallas_call(
        _attn_b_kernel,
        grid=(bsz, n_tok // tq),
        in_specs=[pl.BlockSpec(memory_space=pltpu.SMEM),
                  pl.BlockSpec((1, Q_WIDTH, tq), lambda b, i: (b, 0, i)),
                  keys(CTX_LEN), vals(CTX_LEN), keys(n_tok), vals(n_tok)],
        out_specs=pl.BlockSpec((1, tq, Q_WIDTH), lambda b, i: (b, i, 0)),
        out_shape=jax.ShapeDtypeStruct((bsz, n_tok, Q_WIDTH), BF16),
        scratch_shapes=[pltpu.VMEM((B_KV_HEADS, KV_CHUNK, (B_Q_HEADS // B_KV_HEADS) * tq), F32)] * 2,
        compiler_params=_params(("arbitrary", "arbitrary")),
        name="attn_global",
    )(bound, qt, k_ctx, vt_ctx, k, vt)


def _attn_c_kernel(brange_ref, kn_ref, tab_ref, qt_ref, k0_ref, k1_ref, k2_ref,
                   v0_ref, v1_ref, v2_ref, kc_ref, vc_ref, o_ref, st_ref):
    n_win = tab_ref.shape[2]

    def scores_to(h):
        keys = jnp.concatenate([r[0, h] for r in (k0_ref, k1_ref, k2_ref, kc_ref)], axis=0)
        st = _dot(keys, qt_ref[0, h * HEAD_DIM:(h + 1) * HEAD_DIM, :])
        st_ref[h % 2, 0:n_win] = st[0:n_win] + tab_ref[0, h]
        st_ref[h % 2, n_win:] = st[n_win:]

    def attend(h, ref):
        vals_t = jnp.concatenate([r[0, h] for r in (v0_ref, v1_ref, v2_ref, vc_ref)], axis=1)
        return _softmax_pv_t(st_ref[h % 2], vals_t, ref=ref)

    def run(refs):
        heads_t = []
        scores_to(0)
        for h in range(C_HEADS):
            if h + 1 < C_HEADS:
                scores_to(h + 1)
            heads_t.append(attend(h, refs[h]))
        o_ref[0] = jnp.concatenate(heads_t, axis=0).T.astype(BF16)

    bias_hi, bias_lo = brange_ref[0], brange_ref[1]
    bounds = [_score_bound_t(kn_ref[0, 0, h:h + 1, 0:1], qt_ref[0, h * HEAD_DIM:(h + 1) * HEAD_DIM, :])
              for h in range(C_HEADS)]
    worst = bounds[0]
    for b in bounds[1:]:
        worst = jnp.maximum(worst, b)
    bounded = 2.0 * jnp.max(worst) + (bias_hi - bias_lo) <= MAX_SAFE_SCORE_SPREAD

    @pl.when(bounded)
    def _():
        tq = o_ref.shape[1]
        kv = ((k0_ref, v0_ref), (k1_ref, v1_ref), (k2_ref, v2_ref), (kc_ref, vc_ref))

        def scores_tile(h, s):
            st = _dot(kv[s][0][0, h], qt_ref[0, h * HEAD_DIM:(h + 1) * HEAD_DIM, :])
            if s < 3:
                st = st + tab_ref[0, h, s * tq:(s + 1) * tq, :]
            st_ref[h % 2, s * tq:(s + 1) * tq] = st

        for s in range(4):
            scores_tile(0, s)
        heads_t = []
        for h in range(C_HEADS):
            ref = bounds[h] + bias_hi
            acc = None
            for s in range(4):
                if h + 1 < C_HEADS:
                    scores_tile(h + 1, s)
                p = jnp.exp2(st_ref[h % 2, s * tq:(s + 1) * tq] - ref).astype(BF16)
                part = _dot(kv[s][1][0, h], p)
                acc = part if acc is None else acc + part
            heads_t.append(acc[0:HEAD_DIM] / acc[HEAD_DIM:HEAD_DIM + 1])
        o_ref[0] = jnp.concatenate(heads_t, axis=0).T.astype(BF16)

    @pl.when(jnp.logical_not(bounded))
    def _():
        run([None] * C_HEADS)


def _attention_c(bias_range, kn, tab, qt, k, vt, k_ctx, vt_ctx):
    bsz, _, n_tok = qt.shape
    tq = ROW_TILE
    nt = n_tok // tq
    slot = lambda s: (lambda b, i: jnp.clip(i - 1 + s, 0, nt - 1))
    k_spec = lambda s: pl.BlockSpec((1, C_HEADS, tq, HEAD_DIM), lambda b, i: (b, 0, slot(s)(b, i), 0))
    v_spec = lambda s: pl.BlockSpec((1, C_HEADS, 2 * HEAD_DIM, tq),
                                    lambda b, i: (b, 0, 0, slot(s)(b, i)))
    variant = lambda b, i: (jnp.where(i == 0, 0, jnp.where(i == nt - 1, 2, 1)), 0, 0, 0)
    return pl.pallas_call(
        _attn_c_kernel,
        grid=(bsz, nt),
        in_specs=[pl.BlockSpec(memory_space=pltpu.SMEM),
                  pl.BlockSpec((1, 1, 8, LANES), lambda b, i: (b, i, 0, 0)),
                  pl.BlockSpec((1, C_HEADS, 3 * tq, tq), variant),
                  pl.BlockSpec((1, Q_WIDTH, tq), lambda b, i: (b, 0, i))]
                 + [k_spec(s) for s in range(3)] + [v_spec(s) for s in range(3)]
                 + [pl.BlockSpec((1, C_HEADS, CTX_LEN, HEAD_DIM), lambda b, i: (b, 0, 0, 0)),
                    pl.BlockSpec((1, C_HEADS, 2 * HEAD_DIM, CTX_LEN), lambda b, i: (b, 0, 0, 0))],
        out_specs=pl.BlockSpec((1, tq, Q_WIDTH), lambda b, i: (b, i, 0)),
        out_shape=jax.ShapeDtypeStruct((bsz, n_tok, Q_WIDTH), BF16),
        scratch_shapes=[pltpu.VMEM((2, 3 * tq + CTX_LEN, tq), F32)],
        compiler_params=_params(("arbitrary", "arbitrary")),
        name="attn_nbr",
    )(bias_range, kn, tab, qt, k, k, k, vt, vt, vt, k_ctx, vt_ctx)


def _na_table_kernel(rowmat_ref, o_ref):
    rows = SEQ // GRID_W
    rpt = ROW_TILE // GRID_W
    nt = rows // rpt
    kh = min(NA_KH_MAX, rows)
    masked = jnp.full((GRID_W, GRID_W), NEG_INF, F32)
    for v, t in enumerate((0, 1, nt - 1)):
        for rl in range(rpt):
            r = rpt * t + rl
            rs = min(max(r - kh // 2, 0), rows - kh)
            for sk in range(3 * rpt):
                kr = rpt * (t - 1) + sk
                blk = rowmat_ref[0, kr - r + NA_KH_MAX - 1] if rs <= kr < rs + kh else masked
                o_ref[v, 0, sk * GRID_W:(sk + 1) * GRID_W, rl * GRID_W:(rl + 1) * GRID_W] = blk


def _neighbourhood_tables(rpb):
    c = np.arange(GRID_W)
    ws = np.clip(c - NA_KW // 2, 0, GRID_W - NA_KW)
    valid = (c[None, :] >= ws[:, None]) & (c[None, :] < ws[:, None] + NA_KW)
    coff = np.clip(c[None, :] - c[:, None] + NA_KW - 1, 0, 2 * NA_KW - 2)
    onehot = (coff[:, :, None] == np.arange(2 * NA_KW - 1)).astype(np.float32)
    rowmat = jnp.einsum("hij,ckj->hikc", rpb * LOG2_E, onehot, precision=lax.Precision.HIGHEST)
    rowmat = jnp.where(valid.T, rowmat, NEG_INF).astype(F32)
    n_off = 2 * NA_KH_MAX - 1
    return pl.pallas_call(
        _na_table_kernel,
        grid=(C_HEADS,),
        in_specs=[pl.BlockSpec((1, n_off, GRID_W, GRID_W), lambda h: (h, 0, 0, 0))],
        out_specs=pl.BlockSpec((3, 1, 3 * ROW_TILE, ROW_TILE), lambda h: (0, h, 0, 0)),
        out_shape=jax.ShapeDtypeStruct((3, C_HEADS, 3 * ROW_TILE, ROW_TILE), F32),
        compiler_params=_params(("arbitrary",)),
        name="na_tables",
    )(rowmat)


def _attn_ctx_kernel(sink_ref, qa_ref, ka_ref, va_ref, qb_ref, kb_ref, vb_ref,
                     qc_ref, kc_ref, vc_ref, oa_ref, ob_ref, oc_ref):
    group = A_Q_HEADS // A_KV_HEADS
    n_q = oa_ref.shape[1]
    outs = ([], [], [])
    for h in range(A_Q_HEADS):
        g = h // group
        hs = slice(h * HEAD_DIM, (h + 1) * HEAD_DIM)
        init = _flash_init_t(n_q)
        outs[0].append(_flash_finish_t(
            _flash_update_t(init, ka_ref[0, g], va_ref[0, g], qa_ref[0, hs, :]),
            sink=jnp.full((1, n_q), sink_ref[h], F32)))
        outs[1].append(_flash_finish_t(
            _flash_update_t(init, kb_ref[0, g], vb_ref[0, g], qb_ref[0, hs, :])))
        outs[2].append(_flash_finish_t(
            _flash_update_t(init, kc_ref[0, h], vc_ref[0, h], qc_ref[0, hs, :])))
    for o_ref, heads_t in zip((oa_ref, ob_ref, oc_ref), outs):
        o_ref[0] = jnp.concatenate(heads_t, axis=0).T.astype(BF16)


def _attention_ctx(sink, qa, ka, va, qb, kb, vb, qc, kc, vc):
    bsz = qa.shape[0]
    q_spec = pl.BlockSpec((1, Q_WIDTH, CTX_LEN), lambda b: (b, 0, 0))
    k_spec = lambda n: pl.BlockSpec((1, n, CTX_LEN, HEAD_DIM), lambda b: (b, 0, 0, 0))
    v_spec = lambda n: pl.BlockSpec((1, n, 2 * HEAD_DIM, CTX_LEN), lambda b: (b, 0, 0, 0))
    out = jax.ShapeDtypeStruct((bsz, CTX_LEN, Q_WIDTH), BF16)
    specs = [pl.BlockSpec(memory_space=pltpu.SMEM)]
    for n in (A_KV_HEADS, B_KV_HEADS, C_HEADS):
        specs += [q_spec, k_spec(n), v_spec(n)]
    return pl.pallas_call(
        _attn_ctx_kernel,
        grid=(bsz,),
        in_specs=specs,
        out_specs=[pl.BlockSpec((1, CTX_LEN, Q_WIDTH), lambda b: (b, 0, 0))] * 3,
        out_shape=[out] * 3,
        compiler_params=_params(("arbitrary",)),
        name="attn_ctx",
    )(sink, qa, ka, va, qb, kb, vb, qc, kc, vc)


def _post_kernel(x_ref, oa_ref, ob_ref, oc_ref, sg_ref, mod_ref, norms_ref,
                 wa_ref, wb_ref, wc_ref, wo_ref, w1_ref, w2_ref, o_ref):
    x = x_ref[0]
    merged = None
    for j, (o_r, w_r) in enumerate(((oa_ref, wa_ref), (ob_ref, wb_ref), (oc_ref, wc_ref))):
        gate = sg_ref[0, :, j * D_MODEL:(j + 1) * D_MODEL].astype(F32)
        term = gate * _dot(o_r[0], w_r[...])
        merged = term if merged is None else merged + term
    y = _dot(merged.astype(BF16), wo_ref[...])
    x1 = x + mod_ref[0, 2:3, :] * (_rms(y) * norms_ref[0:1, :])
    h2 = _rms(x1) * norms_ref[1:2, :]
    h2 = (h2 * (1.0 + mod_ref[0, 4:5, :]) + mod_ref[0, 3:4, :]).astype(BF16)
    z = None
    n_chunk = MLP_HIDDEN // D_MODEL
    for j in range(n_chunk):
        cs = slice(j * D_MODEL, (j + 1) * D_MODEL)
        a = jnp.maximum(_dot(h2, w1_ref[:, cs]), 0.0)
        zj = _dot((a * a).astype(BF16), w2_ref[cs, :])
        z = zj if z is None else z + zj
    o_ref[0] = x1 + mod_ref[0, 5:6, :] * (_rms(z) * norms_ref[2:3, :])


def _post_block(x, o_a, o_b, o_c, sg, mod, norms, wa, wb, wc, wo, w1, w2):
    bsz, n_tok, _ = x.shape
    tm = min(DENSE_ROW_TILE, n_tok)
    row = lambda w: pl.BlockSpec((1, tm, w), lambda b, i: (b, i, 0))
    return pl.pallas_call(
        _post_kernel,
        grid=(bsz, n_tok // tm),
        in_specs=[row(D_MODEL), row(Q_WIDTH), row(Q_WIDTH), row(Q_WIDTH), row(3 * D_MODEL),
                  pl.BlockSpec((1, 6, D_MODEL), lambda b, i: (b, 0, 0)),
                  _const_spec((8, D_MODEL)),
                  _const_spec((Q_WIDTH, D_MODEL)), _const_spec((Q_WIDTH, D_MODEL)),
                  _const_spec((Q_WIDTH, D_MODEL)), _const_spec((D_MODEL, D_MODEL)),
                  _const_spec((D_MODEL, MLP_HIDDEN)), _const_spec((MLP_HIDDEN, D_MODEL))],
        out_specs=row(D_MODEL),
        out_shape=jax.ShapeDtypeStruct((bsz, n_tok, D_MODEL), F32),
        compiler_params=_params(("arbitrary", "arbitrary")),
        name="post_mlp",
    )(x, o_a, o_b, o_c, sg, mod, norms, wa, wb, wc, wo, w1, w2)


def _rope_tables(n_tok):
    pos = jnp.arange(n_tok)
    row = (pos // GRID_W).astype(F32)
    col = (pos % GRID_W).astype(F32)
    n_freq = HEAD_DIM // 4
    freqs = ROPE_THETA ** (-jnp.arange(n_freq, dtype=F32) / n_freq)
    ang = jnp.concatenate([row[:, None] * freqs, col[:, None] * freqs], axis=-1)
    cos = jnp.repeat(jnp.cos(ang), 2, axis=-1)
    sign = jnp.tile(jnp.array([-1.0, 1.0], F32), HEAD_DIM // 2)
    sin = jnp.repeat(jnp.sin(ang), 2, axis=-1) * sign
    return jnp.tile(cos, (1, LANES // HEAD_DIM)), jnp.tile(sin, (1, LANES // HEAD_DIM))


def _window_key_norms(kn, kn_ctx):
    prev = jnp.concatenate([kn[:, :1], kn[:, :-1]], axis=1)
    nxt = jnp.concatenate([kn[:, 1:], kn[:, -1:]], axis=1)
    return jnp.maximum(jnp.maximum(prev, kn), jnp.maximum(nxt, kn_ctx))


def kernel(x, c, ctx, c_ctx, w_ada, b_ada, norm_mix_pre, norm_mix_post, w_in, sink_a, qnorm_b,
           knorm_b, rpb_c, w_br_a, w_br_b, w_br_c, w_out, norm_mlp_pre, norm_mlp_post,
           w_mlp_in, w_mlp_out):
    bsz, seq, _ = x.shape
    n_ctx = ctx.shape[1]
    cvec = jnp.concatenate([c, c_ctx[None, :], jnp.zeros((8 - bsz - 1, D_MODEL), F32)], axis=0)
    mod_all = _ada_modulation(cvec, w_ada, b_ada)

    cos_lat, sin_lat = _rope_tables(seq)
    cos_ctx = jnp.ones((n_ctx, LANES), F32)
    sin_ctx = jnp.zeros((n_ctx, LANES), F32)
    head_id = jnp.arange(QK_W) // HEAD_DIM
    gmat = (head_id[:, None] == head_id[None, :]).astype(BF16)
    tile2 = lambda g: jnp.tile(g, LANES // HEAD_DIM)
    swap1 = lambda g: g[jnp.arange(HEAD_DIM) ^ 1]

    x_lat, x_ctx = x, ctx
    for l in range(DEPTH):
        last = l == DEPTH - 1
        mod_lat = mod_all[l, :bsz].reshape(bsz, 6, D_MODEL)
        mod_ctx = jnp.broadcast_to(mod_all[l, bsz].reshape(1, 6, D_MODEL), (bsz, 6, D_MODEL))
        gq, gk = qnorm_b[l], knorm_b[l]
        vecs = jnp.stack([tile2(gq) * Q_SCALE, tile2(swap1(gq)) * Q_SCALE,
                          tile2(gk), tile2(swap1(gk))]
                         + [jnp.zeros((LANES,), F32)] * 4)
        score_bound = (1.02 * HEAD_DIM * Q_SCALE * jnp.max(jnp.abs(gq)) * jnp.max(jnp.abs(gk))
                       ).reshape(1).astype(F32)
        gpre = norm_mix_pre[l][None, :]
        w_ext = w_in[l].astype(BF16)
        norms = jnp.stack([norm_mix_post[l], norm_mlp_pre[l], norm_mlp_post[l]]
                          + [jnp.zeros((D_MODEL,), F32)] * 5)
        wa, wb, wc = (w.astype(BF16) for w in (w_br_a[l], w_br_b[l], w_br_c[l]))
        wo = w_out[l].astype(BF16)
        w1 = w_mlp_in[l].astype(BF16)
        w2 = w_mlp_out[l].astype(BF16)
        tabs = _neighbourhood_tables(rpb_c[l])
        sink = sink_a[l] * LOG2_E

        proj_lat = _input_projection(x_lat, mod_lat, gpre, vecs, cos_lat, sin_lat, gmat, w_ext)
        proj_ctx = _input_projection(x_ctx, mod_ctx, gpre, vecs, cos_ctx, sin_ctx, gmat, w_ext)
        qa, ka, va, qb, kb, vb, qc, kc, vc, sg, na, nc = proj_lat
        qa_c, ka_c, va_c, qb_c, kb_c, vb_c, qc_c, kc_c, vc_c, sg_c, na_c, nc_c = proj_ctx

        o_a = _attention_a(sink, _window_key_norms(na, na_c), qa, ka, va, ka_c, va_c)
        o_b = _attention_b(score_bound, qb, kb, vb, kb_c, vb_c)
        bias = rpb_c[l] * LOG2_E
        bias_range = jnp.stack([jnp.maximum(jnp.max(bias), 0.0), jnp.minimum(jnp.min(bias), 0.0)])
        o_c = _attention_c(bias_range, _window_key_norms(nc, nc_c), tabs, qc, kc, vc, kc_c, vc_c)
        x_lat = _post_block(x_lat, o_a, o_b, o_c, sg, mod_lat, norms, wa, wb, wc, wo, w1, w2)
        if not last:
            o_a_c, o_b_c, o_c_c = _attention_ctx(sink, qa_c, ka_c, va_c, qb_c, kb_c, vb_c,
                                                 qc_c, kc_c, vc_c)
            x_ctx = _post_block(x_ctx, o_a_c, o_b_c, o_c_c, sg_c, mod_ctx, norms,
                                wa, wb, wc, wo, w1, w2)
    return x_lat
```
